```python
import jax, jax.numpy as jnp
from jax import lax
import numpy as np

D_MODEL = 1024
BATCH = 4
SEQ = 8192
DEPTH = 2

GRID_W = 64
CTX_LEN = 256
ROPE_BASE = 10000.0
EPS = 1e-6
NEG = -1e30
N_MOD = 9

D_GROUP = D_MODEL // 4
MLA_HEADS = 4
MLA_NOPE = 64
MLA_ROPE = 32
MLA_V = 64
MLA_Q_RANK = 192
MLA_KV_RANK = 128
SC_WIDTH = D_GROUP
SC_K = 3
WA_HEADS = 4
WA_KV_HEADS = 2
WA_HEAD_DIM = 64
WINDOW = 128
BLOCK = 128
CF_WIDTH = D_GROUP
CF_K = 31
D_FF = 2816

IN_SIZES = (MLA_Q_RANK, MLA_KV_RANK, MLA_ROPE, 3 * SC_WIDTH,
            WA_HEADS * WA_HEAD_DIM, 2 * WA_KV_HEADS * WA_HEAD_DIM, 2 * CF_WIDTH)
D_IN = sum(IN_SIZES)
D_MIX = MLA_HEADS * MLA_V + SC_WIDTH + WA_HEADS * WA_HEAD_DIM + CF_WIDTH

kernel_name = "hybrid_parallel_group_dit_block"


def rms_norm(x, g):
    xf = x.astype(jnp.float32)
    y = xf * lax.rsqrt(jnp.mean(xf * xf, axis=-1, keepdims=True) + EPS)
    return (y * g.astype(jnp.float32)).astype(x.dtype)


def layer_norm(x, g, b):
    xf = x.astype(jnp.float32)
    mu = jnp.mean(xf, axis=-1, keepdims=True)
    var = jnp.mean(jnp.square(xf - mu), axis=-1, keepdims=True)
    y = (xf - mu) * lax.rsqrt(var + EPS)
    return (y * g.astype(jnp.float32) + b.astype(jnp.float32)).astype(x.dtype)


def modulate(h, shift, scale):
    return h * (1 + scale) + shift


def swiglu(h, w_gate, w_up, w_down):
    return (jax.nn.silu(h @ w_gate) * (h @ w_up)) @ w_down


def split_in(z):
    idx = tuple(int(i) for i in np.cumsum(IN_SIZES)[:-1])
    return jnp.split(z, idx, axis=-1)


def dw_conv(x, w):
    k = w.shape[0]
    return lax.conv_general_dilated(
        x, w[:, None, :].astype(x.dtype), (1,), [(k // 2, k // 2)],
        dimension_numbers=('NWC', 'WIO', 'NWC'), feature_group_count=x.shape[-1])


def axial_tables(row_pos, col_pos, d_rot):
    d_ax = d_rot // 2
    inv = ROPE_BASE ** (-jnp.arange(0, d_ax, 2, dtype=jnp.float32) / d_ax)
    ar = row_pos[:, None] * inv[None, :]
    ac = col_pos[:, None] * inv[None, :]
    return (jnp.cos(ar), jnp.sin(ar), jnp.cos(ac), jnp.sin(ac))


def _rotate(x, cos, sin):
    x1, x2 = jnp.split(x, 2, axis=-1)
    cos = cos[:, None, :]
    sin = sin[:, None, :]
    return jnp.concatenate([x1 * cos - x2 * sin, x2 * cos + x1 * sin], axis=-1)


def axial_rope(x, tab):
    cr, sr, cc, sc = tab
    xr, xc = jnp.split(x, 2, axis=-1)
    return jnp.concatenate([_rotate(xr, cr, sr), _rotate(xc, cc, sc)], axis=-1).astype(x.dtype)


def mla_q(cq, g_q, w_uq, tab):
    b, l, _ = cq.shape
    q = (rms_norm(cq, g_q) @ w_uq).reshape(b, l, MLA_HEADS, MLA_NOPE + MLA_ROPE)
    q_nope, q_rope = q[..., :MLA_NOPE], q[..., MLA_NOPE:]
    if tab is not None:
        q_rope = axial_rope(q_rope, tab)
    return jnp.concatenate([q_nope, q_rope], axis=-1)


def mla_kv(ckv, kr, g_kv, w_ukv, tab):
    b, l, _ = ckv.shape
    kv = (rms_norm(ckv, g_kv) @ w_ukv).reshape(b, l, MLA_HEADS, MLA_NOPE + MLA_V)
    k_nope, v = kv[..., :MLA_NOPE], kv[..., MLA_NOPE:]
    k_rope = kr[:, :, None, :]
    if tab is not None:
        k_rope = axial_rope(k_rope, tab)
    k = jnp.concatenate([k_nope, jnp.broadcast_to(k_rope, (b, l, MLA_HEADS, MLA_ROPE))], axis=-1)
    return k, v


def dense_attend(q, k, v):
    scale = q.shape[-1] ** -0.5
    s = jnp.einsum('bqhd,bkhd->bhqk', q, k).astype(jnp.float32) * scale
    p = jax.nn.softmax(s, axis=-1).astype(v.dtype)
    o = jnp.einsum('bhqk,bkhd->bqhd', p, v)
    return o.reshape(o.shape[0], o.shape[1], -1)


def dense_attend_blocked(q, k, v):
    b, s, h, dq = q.shape
    nb = s // BLOCK
    qb = jnp.moveaxis(q.reshape(b, nb, BLOCK, h, dq), 1, 0)
    scale = dq ** -0.5

    def one(qblk):
        sc = jnp.einsum('bqhd,bkhd->bhqk', qblk, k).astype(jnp.float32) * scale
        p = jax.nn.softmax(sc, axis=-1).astype(v.dtype)
        return jnp.einsum('bhqk,bkhd->bqhd', p, v)

    o = lax.map(one, qb)
    return jnp.moveaxis(o, 0, 1).reshape(b, s, -1)


def short_conv_mix(z, w_conv):
    b_g, c_g, xin = jnp.split(z, 3, axis=-1)
    return b_g * dw_conv(c_g * xin, w_conv)


def wa_q(zq, tab):
    b, l, _ = zq.shape
    q = zq.reshape(b, l, WA_HEADS, WA_HEAD_DIM)
    return axial_rope(q, tab) if tab is not None else q


def wa_kv(zkv, tab):
    b, l, _ = zkv.shape
    k, v = jnp.split(zkv.reshape(b, l, 2 * WA_KV_HEADS, WA_HEAD_DIM), 2, axis=2)
    if tab is not None:
        k = axial_rope(k, tab)
    return k, v


def window_attend_latent(q, k, v, k_ctx, v_ctx, sink):
    b, s, hq, d = q.shape
    hkv = k.shape[2]
    g = hq // hkv
    nb = s // BLOCK
    c = k_ctx.shape[1]

    def band(t):
        tp = jnp.pad(t, ((0, 0), (BLOCK, BLOCK), (0, 0), (0, 0))).reshape(b, nb + 2, BLOCK, hkv, d)
        return jnp.concatenate([tp[:, :-2], tp[:, 1:-1], tp[:, 2:]], axis=2)

    kb, vb = band(k), band(v)
    qb = q.reshape(b, nb, BLOCK, hkv, g, d)
    scale = d ** -0.5
    s_loc = jnp.einsum('bnqhgd,bnkhd->bnhgqk', qb, kb).astype(jnp.float32) * scale
    qpos = jnp.arange(nb)[:, None, None] * BLOCK + jnp.arange(BLOCK)[None, :, None]
    kpos = (jnp.arange(nb)[:, None, None] - 1) * BLOCK + jnp.arange(3 * BLOCK)[None, None, :]
    valid = (jnp.abs(kpos - qpos) <= WINDOW) & (kpos >= 0) & (kpos < s)
    s_loc = jnp.where(valid[None, :, None, None], s_loc, NEG)
    s_ctx = jnp.einsum('bnqhgd,bchd->bnhgqc', qb, k_ctx).astype(jnp.float32) * scale
    s_sink = jnp.broadcast_to(sink.astype(jnp.float32).reshape(1, 1, hkv, g, 1, 1),
                              s_ctx.shape[:-1] + (1,))
    p = jax.nn.softmax(jnp.concatenate([s_loc, s_ctx, s_sink], axis=-1), axis=-1)
    p_loc = p[..., :3 * BLOCK].astype(v.dtype)
    p_ctx = p[..., 3 * BLOCK:3 * BLOCK + c].astype(v.dtype)
    o = (jnp.einsum('bnhgqk,bnkhd->bnqhgd', p_loc, vb)
         + jnp.einsum('bnhgqc,bchd->bnqhgd', p_ctx, v_ctx))
    return o.reshape(b, s, hq * d)


def ctx_gqa_sink(q, k, v, sink):
    b, c, hq, d = q.shape
    hkv = k.shape[2]
    g = hq // hkv
    qg = q.reshape(b, c, hkv, g, d)
    s = jnp.einsum('bqhgd,bkhd->bhgqk', qg, k).astype(jnp.float32) * (d ** -0.5)
    s_sink = jnp.broadcast_to(sink.astype(jnp.float32).reshape(1, hkv, g, 1, 1), s.shape[:-1] + (1,))
    p = jax.nn.softmax(jnp.concatenate([s, s_sink], axis=-1), axis=-1)[..., :c].astype(v.dtype)
    o = jnp.einsum('bhgqk,bkhd->bqhgd', p, v)
    return o.reshape(b, c, hq * d)


def conformer_conv_mix(z, w_conv, b_conv, g_ln, b_ln):
    a, gt = jnp.split(z, 2, axis=-1)
    u = a * jax.nn.sigmoid(gt)
    u = dw_conv(u, w_conv) + b_conv
    return jax.nn.silu(layer_norm(u, g_ln, b_ln))


def setup_inputs(seed: int = 0) -> dict:
    key = jax.random.key(seed)
    ks = iter(jax.random.split(key, 40))
    f32 = jnp.float32

    def nrm(shape, scale):
        return jax.random.normal(next(ks), shape, f32) * scale

    def gain(shape):
        return 1.0 + 0.05 * jax.random.normal(next(ks), shape, f32)

    L, D = DEPTH, D_MODEL
    return {
        "x": nrm((BATCH, SEQ, D), 1.0),
        "c": nrm((BATCH, D), 1.0),
        "ctx": nrm((BATCH, CTX_LEN, D), 1.0),
        "c_ctx": nrm((D,), 1.0),
        "w_mod": nrm((L, D, N_MOD * D), D ** -0.5),
        "b_mod": nrm((L, N_MOD * D), 0.02),
        "g_ffn1": gain((L, D)),
        "w1_gate": nrm((L, D, D_FF), D ** -0.5),
        "w1_up": nrm((L, D, D_FF), D ** -0.5),
        "w1_down": nrm((L, D_FF, D), D_FF ** -0.5),
        "g_mix": gain((L, D)),
        "w_in": nrm((L, D, D_IN), D ** -0.5),
        "g_mla_q": gain((L, MLA_Q_RANK)),
        "w_mla_uq": nrm((L, MLA_Q_RANK, MLA_HEADS * (MLA_NOPE + MLA_ROPE)), MLA_Q_RANK ** -0.5),
        "g_mla_kv": gain((L, MLA_KV_RANK)),
        "w_mla_ukv": nrm((L, MLA_KV_RANK, MLA_HEADS * (MLA_NOPE + MLA_V)), MLA_KV_RANK ** -0.5),
        "w_sc_conv": nrm((L, SC_K, SC_WIDTH), SC_K ** -0.5),
        "wa_sink": nrm((L, WA_HEADS), 0.5),
        "w_cf_conv": nrm((L, CF_K, CF_WIDTH), CF_K ** -0.5),
        "b_cf_conv": nrm((L, CF_WIDTH), 0.02),
        "g_cf_ln": gain((L, CF_WIDTH)),
        "b_cf_ln": nrm((L, CF_WIDTH), 0.02),
        "w_out": nrm((L, D_MIX, D), D_MIX ** -0.5),
        "g_ffn2": gain((L, D)),
        "w2_gate": nrm((L, D, D_FF), D ** -0.5),
        "w2_up": nrm((L, D, D_FF), D ** -0.5),
        "w2_down": nrm((L, D_FF, D), D_FF ** -0.5),
        "g_final": gain((D,)),
    }


def reference(x, c, ctx, c_ctx, w_mod, b_mod, g_ffn1, w1_gate, w1_up, w1_down, g_mix, w_in,
              g_mla_q, w_mla_uq, g_mla_kv, w_mla_ukv, w_sc_conv, wa_sink, w_cf_conv, b_cf_conv,
              g_cf_ln, b_cf_ln, w_out, g_ffn2, w2_gate, w2_up, w2_down, g_final):
    b, s, d = x.shape
    rows = s // GRID_W
    row_pos = jnp.repeat(jnp.arange(rows), GRID_W).astype(jnp.float32)
    col_pos = jnp.tile(jnp.arange(GRID_W), rows).astype(jnp.float32)
    tab_mla = axial_tables(row_pos, col_pos, MLA_ROPE)
    tab_wa = axial_tables(row_pos, col_pos, WA_HEAD_DIM)

    h_lat, h_ctx = x, ctx
    for l in range(DEPTH):
        last = l == DEPTH - 1
        m_l = (jax.nn.silu(c) @ w_mod[l] + b_mod[l]).reshape(b, N_MOD, d).transpose(1, 0, 2)[:, :, None, :]
        m_c = (jax.nn.silu(c_ctx) @ w_mod[l] + b_mod[l]).reshape(N_MOD, 1, 1, d)

        h_lat = h_lat + 0.5 * m_l[2] * swiglu(modulate(rms_norm(h_lat, g_ffn1[l]), m_l[0], m_l[1]),
                                              w1_gate[l], w1_up[l], w1_down[l])
        h_ctx = h_ctx + 0.5 * m_c[2] * swiglu(modulate(rms_norm(h_ctx, g_ffn1[l]), m_c[0], m_c[1]),
                                              w1_gate[l], w1_up[l], w1_down[l])

        z_lat = split_in(modulate(rms_norm(h_lat, g_mix[l]), m_l[3], m_l[4]) @ w_in[l])
        z_ctx = split_in(modulate(rms_norm(h_ctx, g_mix[l]), m_c[3], m_c[4]) @ w_in[l])

        q_a = mla_q(z_lat[0], g_mla_q[l], w_mla_uq[l], tab_mla)
        k_a, v_a = mla_kv(z_lat[1], z_lat[2], g_mla_kv[l], w_mla_ukv[l], tab_mla)
        kc_a, vc_a = mla_kv(z_ctx[1], z_ctx[2], g_mla_kv[l], w_mla_ukv[l], None)
        o_a = dense_attend_blocked(q_a, jnp.concatenate([k_a, kc_a], axis=1),
                                   jnp.concatenate([v_a, vc_a], axis=1))
        o_b = short_conv_mix(z_lat[3], w_sc_conv[l])
        q_w = wa_q(z_lat[4], tab_wa)
        k_w, v_w = wa_kv(z_lat[5], tab_wa)
        kc_w, vc_w = wa_kv(z_ctx[5], None)
        o_c = window_attend_latent(q_w, k_w, v_w, kc_w, vc_w, wa_sink[l])
        o_d = conformer_conv_mix(z_lat[6], w_cf_conv[l], b_cf_conv[l], g_cf_ln[l], b_cf_ln[l])
        h_lat = h_lat + m_l[5] * (jnp.concatenate([o_a, o_b, o_c, o_d], axis=-1) @ w_out[l])

        if not last:
            oc_a = dense_attend(mla_q(z_ctx[0], g_mla_q[l], w_mla_uq[l], None), kc_a, vc_a)
            oc_b = short_conv_mix(z_ctx[3], w_sc_conv[l])
            oc_c = ctx_gqa_sink(wa_q(z_ctx[4], None), kc_w, vc_w, wa_sink[l])
            oc_d = conformer_conv_mix(z_ctx[6], w_cf_conv[l], b_cf_conv[l], g_cf_ln[l], b_cf_ln[l])
            h_ctx = h_ctx + m_c[5] * (jnp.concatenate([oc_a, oc_b, oc_c, oc_d], axis=-1) @ w_out[l])
            h_ctx = h_ctx + 0.5 * m_c[8] * swiglu(modulate(rms_norm(h_ctx, g_ffn2[l]), m_c[6], m_c[7]),
                                                  w2_gate[l], w2_up[l], w2_down[l])

        h_lat = h_lat + 0.5 * m_l[8] * swiglu(modulate(rms_norm(h_lat, g_ffn2[l]), m_l[6], m_l[7]),
                                              w2_gate[l], w2_up[l], w2_down[l])

    return rms_norm(h_lat, g_final)
```

```python
import functools

import numpy as np
import jax
import jax.numpy as jnp
from jax import lax
from jax.experimental import pallas as pl
from jax.experimental.pallas import tpu as pltpu

GRID_W = 64
ROPE_BASE = 10000.0
EPS = 1e-6
NEG = -1e30
N_MOD = 9
LOG2E = 1.4426950408889634

MLA_HEADS = 4
MLA_NOPE = 64
MLA_ROPE = 32
MLA_V = 64
MLA_Q_RANK = 192
MLA_KV_RANK = 128
SC_WIDTH = 256
SC_K = 3
WA_HEADS = 4
WA_KV_HEADS = 2
WA_HEAD_DIM = 64
WINDOW = 128
CF_WIDTH = 256
CF_K = 31
IN_SIZES = (MLA_Q_RANK, MLA_KV_RANK, MLA_ROPE, 3 * SC_WIDTH,
            WA_HEADS * WA_HEAD_DIM, 2 * WA_KV_HEADS * WA_HEAD_DIM, 2 * CF_WIDTH)

LANES = 128
MXU_N = 256

_BF = jnp.bfloat16
_F32 = jnp.float32

SEG_CQ = 0
SEG_KR = 256
SEG_SC = 512
SEG_WQ = 1280
SEG_CF = 1792
SEG_WK = 2304
SEG_WV = 2560
SEG_CKV = 2688
N_PACK = 2816

TM_LAT = 512
TQ_MLA = 256
TK_MLA = 512
TQ_WA = 256
FF_CHUNK = 256
VMEM_LIMIT = 52 * 1024 * 1024


def _params(n_axes):
    return pltpu.CompilerParams(dimension_semantics=("arbitrary",) * n_axes,
                                vmem_limit_bytes=VMEM_LIMIT)


def _const_spec(shape):
    nd = len(shape)
    return pl.BlockSpec(shape, lambda *_: (0,) * nd, pipeline_mode=pl.Buffered(1))


def _dot(a, b):
    return jnp.dot(a, b, preferred_element_type=_F32)


def _dot_nt(a, b):
    return lax.dot_general(a, b, (((1,), (1,)), ((), ())), preferred_element_type=_F32)


def _norm_mod(x, g, shift, scale):
    y = x * lax.rsqrt(jnp.mean(x * x, axis=-1, keepdims=True) + EPS) * g
    return y * (1.0 + scale) + shift


def _mod_kernel(c_ref, w_ref, b_ref, o_ref):
    c = c_ref[...]
    a = c * jax.nn.sigmoid(c)
    o_ref[0] = jnp.dot(a, w_ref[0], preferred_element_type=_F32,
                       precision=lax.Precision.HIGHEST) + b_ref[0]


def _modulation(cs, w_mod, b_mod):
    depth, d, n = w_mod.shape
    tn = 1024
    rows = cs.shape[0]
    return pl.pallas_call(
        _mod_kernel,
        out_shape=jax.ShapeDtypeStruct((depth, rows, n), _F32),
        grid=(depth, n // tn),
        in_specs=[pl.BlockSpec((rows, d), lambda l, j: (0, 0)),
                  pl.BlockSpec((1, d, tn), lambda l, j: (l, 0, j)),
                  pl.BlockSpec((1, 1, tn), lambda l, j: (l, 0, j))],
        out_specs=pl.BlockSpec((1, rows, tn), lambda l, j: (l, 0, j)),
        compiler_params=_params(2),
        name="modulation",
    )(cs, w_mod, b_mod.reshape(depth, 1, n))


def _ffn_kernel(h_ref, mod_ref, g_ref, wg_ref, wu_ref, wd_ref, *rest, which, final):
    if final:
        gf_ref, o_ref, a_ref = rest
    else:
        o_ref, a_ref = rest
    x = h_ref[...]
    shift = mod_ref[0, 3 * which:3 * which + 1, :]
    scale = mod_ref[0, 3 * which + 1:3 * which + 2, :]
    gate = mod_ref[0, 3 * which + 2:3 * which + 3, :]
    xb = _norm_mod(x, g_ref[...], shift, scale).astype(_BF)
    d_ff = wg_ref.shape[1]
    for j in range(d_ff // FF_CHUNK):
        sl = slice(j * FF_CHUNK, (j + 1) * FF_CHUNK)
        gt = _dot(xb, wg_ref[:, sl])
        up = _dot(xb, wu_ref[:, sl])
        a_ref[:, sl] = (gt * jax.nn.sigmoid(gt) * up).astype(_BF)
    y = x + 0.5 * gate * _dot(a_ref[...], wd_ref[...])
    if final:
        y = y * lax.rsqrt(jnp.mean(y * y, axis=-1, keepdims=True) + EPS) * gf_ref[...]
    o_ref[...] = y


def _ffn(h, mod, g, wg, wu, wd, *, which, tm, tiles_per_mod, mod_base, g_final=None):
    t, d = h.shape
    d_ff = wg.shape[1]
    final = g_final is not None
    if tiles_per_mod is None:
        mod_map = lambda i: (mod_base, 0, 0)
    else:
        mod_map = lambda i: (i // tiles_per_mod, 0, 0)
    in_specs = [pl.BlockSpec((tm, d), lambda i: (i, 0)),
                pl.BlockSpec((1, N_MOD, d), mod_map),
                _const_spec((1, d)),
                _const_spec((d, d_ff)), _const_spec((d, d_ff)), _const_spec((d_ff, d))]
    args = [h, mod, g, wg, wu, wd]
    if final:
        in_specs.append(_const_spec((1, d)))
        args.append(g_final)
    return pl.pallas_call(
        functools.partial(_ffn_kernel, which=which, final=final),
        out_shape=jax.ShapeDtypeStruct((t, d), _F32),
        grid=(t // tm,),
        in_specs=in_specs,
        out_specs=pl.BlockSpec((tm, d), lambda i: (i, 0)),
        scratch_shapes=[pltpu.VMEM((tm, d_ff), _BF)],
        compiler_params=_params(1),
        name="ffn",
    )(*args)


def _in_proj_kernel(h_ref, mod_ref, g_ref, win_ref, gq_ref, wuq_ref, gkv_ref, wukv_ref, *rest, rope):
    if rope:
        ca_ref, sa_ref, cw_ref, sw_ref = rest[:4]
        rest = rest[4:]
    qa_ref, ka_ref, vat_ref, bg_ref, cx_ref, qw_ref, kw_ref, vwt_ref, ud_ref = rest

    x = h_ref[...]
    xb = _norm_mod(x, g_ref[...], mod_ref[0, 3:4, :], mod_ref[0, 4:5, :]).astype(_BF)

    def seg(lo, width):
        return _dot(xb, win_ref[:, lo:lo + width])

    cq = seg(SEG_CQ, 256)
    cqn = cq * lax.rsqrt(jnp.sum(cq * cq, axis=-1, keepdims=True) * (1.0 / MLA_Q_RANK) + EPS) * gq_ref[...]
    qq = _dot(cqn.astype(_BF), wuq_ref[...])
    q = qq[:, :MLA_HEADS * LANES]
    if rope:
        ca = ca_ref[...]
        sa = sa_ref[...]
        q = (q * jnp.concatenate([ca] * MLA_HEADS, axis=1)
             + qq[:, MLA_HEADS * LANES:] * jnp.concatenate([sa] * MLA_HEADS, axis=1))
    qa_ref[...] = q.astype(_BF)

    ckv = seg(SEG_CKV, 128)
    ckvn = ckv * lax.rsqrt(jnp.mean(ckv * ckv, axis=-1, keepdims=True) + EPS) * gkv_ref[...]
    kv = _dot(ckvn.astype(_BF), wukv_ref[...])
    kr2 = seg(SEG_KR, 256)
    kr = kr2[:, :LANES]
    if rope:
        kr = kr * ca + kr2[:, LANES:] * sa
    ka_ref[...] = (kv[:, :MLA_HEADS * LANES] + jnp.concatenate([kr] * MLA_HEADS, axis=1)).astype(_BF)
    vat_ref[...] = kv[:, MLA_HEADS * LANES:].T.astype(_BF)

    sc = seg(SEG_SC, 768)
    bg_ref[...] = sc[:, :256]
    cx_ref[...] = sc[:, 256:512] * sc[:, 512:]

    wq2 = seg(SEG_WQ, 512)
    qw = wq2[:, :256]
    wk2 = seg(SEG_WK, 256)
    kw = wk2[:, :LANES]
    if rope:
        cw = cw_ref[...]
        sw = sw_ref[...]
        qw = qw * jnp.concatenate([cw, cw], axis=1) + wq2[:, 256:] * jnp.concatenate([sw, sw], axis=1)
        kw = kw * cw + wk2[:, LANES:] * sw
    qw_ref[...] = qw.astype(_BF)
    lane = lax.broadcasted_iota(jnp.int32, kw.shape, 1)
    low = lane < WA_HEAD_DIM
    kw_r = pltpu.roll(kw, WA_HEAD_DIM, axis=1)
    zero = jnp.zeros_like(kw)
    kw_ref[...] = jnp.concatenate(
        [jnp.where(low, kw, zero), jnp.where(low, zero, kw_r),
         jnp.where(low, kw_r, zero), jnp.where(low, zero, kw)], axis=1).astype(_BF)
    vwt_ref[...] = seg(SEG_WV, 128).T.astype(_BF)

    cf = seg(SEG_CF, 512)
    ud_ref[...] = cf[:, :256] * jax.nn.sigmoid(cf[:, 256:])


def _in_proj(h, mod, g, w, tabs, *, tm, tiles_per_mod, mod_base, seq_tiles):
    t, d = h.shape
    rope = tabs is not None
    if tiles_per_mod is None:
        mod_map = lambda i: (mod_base, 0, 0)
    else:
        mod_map = lambda i: (i // tiles_per_mod, 0, 0)
    in_specs = [pl.BlockSpec((tm, d), lambda i: (i, 0)),
                pl.BlockSpec((1, N_MOD, d), mod_map),
                _const_spec((1, d)),
                _const_spec((d, N_PACK)),
                _const_spec((1, 256)), _const_spec((256, 1024)),
                _const_spec((1, 128)), _const_spec((128, 768))]
    args = [h, mod, g, w["w_in"], w["g_q"], w["w_uq"], w["g_kv"], w["w_ukv"]]
    if rope:
        in_specs += [pl.BlockSpec((tm, LANES), lambda i: (i % seq_tiles, 0))] * 4
        args += list(tabs)
    row = lambda width: pl.BlockSpec((tm, width), lambda i: (i, 0))
    col = lambda height: pl.BlockSpec((height, tm), lambda i: (0, i))
    out_shape = (
        jax.ShapeDtypeStruct((t, 512), _BF),
        jax.ShapeDtypeStruct((t, 512), _BF),
        jax.ShapeDtypeStruct((256, t), _BF),
        jax.ShapeDtypeStruct((t, 256), _F32),
        jax.ShapeDtypeStruct((t, 256), _F32),
        jax.ShapeDtypeStruct((t, 256), _BF),
        jax.ShapeDtypeStruct((t, 512), _BF),
        jax.ShapeDtypeStruct((128, t), _BF),
        jax.ShapeDtypeStruct((t, 256), _F32),
    )
    out_specs = (row(512), row(512), col(256), row(256), row(256), row(256), row(512), col(128), row(256))
    return pl.pallas_call(
        functools.partial(_in_proj_kernel, rope=rope),
        out_shape=out_shape,
        grid=(t // tm,),
        in_specs=in_specs,
        out_specs=out_specs,
        compiler_params=_params(1),
        name="in_proj",
    )(*args)


def _mla_kernel(q_ref, *refs, has_lat, n_lat_tiles):
    if has_lat:
        kl_ref, vl_ref, kc_ref, vc_ref, o_ref = refs
    else:
        kc_ref, vc_ref, o_ref = refs
    tq = q_ref.shape[0]
    outs = []
    for h in range(MLA_HEADS):
        lanes = slice(h * LANES, (h + 1) * LANES)
        rows = slice(h * MLA_V, (h + 1) * MLA_V)
        qh = q_ref[:, lanes]

        def step(k_t, vt_t, carry, qh=qh):
            m, l, acc = carry
            s = _dot_nt(k_t, qh)
            m_new = jnp.maximum(m, jnp.max(s, axis=0, keepdims=True))
            alpha = jnp.exp2(m - m_new)
            p = jnp.exp2(s - m_new)
            l = alpha * l + jnp.sum(p, axis=0, keepdims=True)
            acc = alpha * acc + _dot(vt_t, p.astype(_BF))
            return m_new, l, acc

        carry = (jnp.full((1, tq), NEG, _F32), jnp.zeros((1, tq), _F32), jnp.zeros((MLA_V, tq), _F32))
        if has_lat:
            def body(t, carry, lanes=lanes, rows=rows, step=step):
                off = pl.multiple_of(t * TK_MLA, TK_MLA)
                return step(kl_ref[pl.ds(off, TK_MLA), lanes], vl_ref[rows, pl.ds(off, TK_MLA)], carry)
            carry = lax.fori_loop(0, n_lat_tiles, body, carry)
        m, l, acc = step(kc_ref[:, lanes], vc_ref[rows, :], carry)
        outs.append(acc / l)
    o_ref[...] = jnp.concatenate(outs, axis=0).T.astype(_BF)


def _mla_attn(q, k_lat, vt_lat, k_ctx, vt_ctx, *, batch, tq):
    t = q.shape[0]
    per_b = t // batch
    nq = per_b // tq
    c = k_ctx.shape[0] // batch
    has_lat = k_lat is not None
    in_specs = [pl.BlockSpec((tq, 512), lambda b, j: (b * nq + j, 0))]
    args = [q]
    n_lat_tiles = 0
    if has_lat:
        s = k_lat.shape[0] // batch
        n_lat_tiles = s // TK_MLA
        in_specs += [pl.BlockSpec((s, 512), lambda b, j: (b, 0)),
                     pl.BlockSpec((256, s), lambda b, j: (0, b))]
        args += [k_lat, vt_lat]
    in_specs += [pl.BlockSpec((c, 512), lambda b, j: (b, 0)),
                 pl.BlockSpec((256, c), lambda b, j: (0, b))]
    args += [k_ctx, vt_ctx]
    return pl.pallas_call(
        functools.partial(_mla_kernel, has_lat=has_lat, n_lat_tiles=n_lat_tiles),
        out_shape=jax.ShapeDtypeStruct((t, 256), _BF),
        grid=(batch, nq),
        in_specs=in_specs,
        out_specs=pl.BlockSpec((tq, 256), lambda b, j: (b * nq + j, 0)),
        compiler_params=_params(2),
        name="mla_attn",
    )(*args)


def _wa_kernel(sink_ref, q_ref, *refs, has_lat, seq_len):
    if has_lat:
        kl_ref, vl_ref, kc_ref, vc_ref, o_ref = refs
    else:
        kc_ref, vc_ref, o_ref = refs
    tq = q_ref.shape[0]
    win_keys = tq + 2 * WINDOW
    if has_lat:
        q0 = pl.program_id(1) * tq
        start = pl.multiple_of(jnp.clip(q0 - WINDOW, 0, seq_len - win_keys), LANES)
        kpos = start + lax.broadcasted_iota(jnp.int32, (win_keys, tq), 0)
        qpos = q0 + lax.broadcasted_iota(jnp.int32, (win_keys, tq), 1)
        valid = jnp.abs(kpos - qpos) <= WINDOW
    outs = []
    for hq in range(WA_HEADS):
        g = hq // (WA_HEADS // WA_KV_HEADS)
        lanes = slice(hq * LANES, (hq + 1) * LANES)
        rows = slice(g * WA_HEAD_DIM, (g + 1) * WA_HEAD_DIM)
        qpair = q_ref[:, g * LANES:(g + 1) * LANES]
        snk = sink_ref[hq] * LOG2E
        s_ctx = _dot_nt(kc_ref[:, lanes], qpair)
        m = jnp.maximum(jnp.max(s_ctx, axis=0, keepdims=True), snk)
        if has_lat:
            s_loc = jnp.where(valid, _dot_nt(kl_ref[pl.ds(start, win_keys), lanes], qpair), NEG)
            m = jnp.maximum(m, jnp.max(s_loc, axis=0, keepdims=True))
        p_ctx = jnp.exp2(s_ctx - m)
        l = jnp.sum(p_ctx, axis=0, keepdims=True) + jnp.exp2(snk - m)
        o_t = _dot(vc_ref[rows, :], p_ctx.astype(_BF))
        if has_lat:
            p_loc = jnp.exp2(s_loc - m)
            l = l + jnp.sum(p_loc, axis=0, keepdims=True)
            o_t = o_t + _dot(vl_ref[rows, pl.ds(start, win_keys)], p_loc.astype(_BF))
        outs.append(o_t / l)
    o_ref[...] = jnp.concatenate(outs, axis=0).T.astype(_BF)


def _wa_attn(sink, q, k_lat, vt_lat, k_ctx, vt_ctx, *, batch, tq):
    t = q.shape[0]
    per_b = t // batch
    nq = per_b // tq
    c = k_ctx.shape[0] // batch
    has_lat = k_lat is not None
    in_specs = [pl.BlockSpec(memory_space=pltpu.SMEM),
                pl.BlockSpec((tq, 256), lambda b, j: (b * nq + j, 0))]
    args = [sink, q]
    seq_len = 0
    if has_lat:
        seq_len = k_lat.shape[0] // batch
        in_specs += [pl.BlockSpec((seq_len, 512), lambda b, j: (b, 0)),
                     pl.BlockSpec((128, seq_len), lambda b, j: (0, b))]
        args += [k_lat, vt_lat]
    in_specs += [pl.BlockSpec((c, 512), lambda b, j: (b, 0)),
                 pl.BlockSpec((128, c), lambda b, j: (0, b))]
    args += [k_ctx, vt_ctx]
    return pl.pallas_call(
        functools.partial(_wa_kernel, has_lat=has_lat, seq_len=seq_len),
        out_shape=jax.ShapeDtypeStruct((t, 256), _BF),
        grid=(batch, nq),
        in_specs=in_specs,
        out_specs=pl.BlockSpec((tq, 256), lambda b, j: (b * nq + j, 0)),
        compiler_params=_params(2),
        name="wa_attn",
    )(*args)


HALO_SC = 8
HALO_CF = 16
CONV_ROWS = 64


def _mix_out_kernel(h_ref, mod_ref, oa_ref, oc_ref, bg_ref, cx_ref, ud_ref, *rest, halo, seq_tiles):
    if halo:
        cxp_ref, cxn_ref, udp_ref, udn_ref = rest[:4]
        rest = rest[4:]
    wsc_ref, wcf_ref, bcf_ref, gln_ref, bln_ref, wout_ref, o_ref, xsc_ref, xcf_ref, ob_ref, od_ref = rest
    tm = h_ref.shape[0]
    width = cx_ref.shape[1]

    if halo:
        i = pl.program_id(0) % seq_tiles
        has_prev = (i != 0).astype(_F32)
        has_next = (i != seq_tiles - 1).astype(_F32)
        xsc_ref[0:HALO_SC, :] = cxp_ref[...] * has_prev
        xsc_ref[HALO_SC + tm:, :] = cxn_ref[...] * has_next
        xcf_ref[0:HALO_CF, :] = udp_ref[...] * has_prev
        xcf_ref[HALO_CF + tm:, :] = udn_ref[...] * has_next
    else:
        xsc_ref[0:HALO_SC, :] = jnp.zeros((HALO_SC, width), _F32)
        xsc_ref[HALO_SC + tm:, :] = jnp.zeros((HALO_SC, width), _F32)
        xcf_ref[0:HALO_CF, :] = jnp.zeros((HALO_CF, width), _F32)
        xcf_ref[HALO_CF + tm:, :] = jnp.zeros((HALO_CF, width), _F32)
    xsc_ref[HALO_SC:HALO_SC + tm, :] = cx_ref[...]
    xcf_ref[HALO_CF:HALO_CF + tm, :] = ud_ref[...]

    for r0 in range(0, tm, CONV_ROWS):
        acc = jnp.zeros((CONV_ROWS, width), _F32)
        for k in range(SC_K):
            off = HALO_SC + r0 + k - SC_K // 2
            acc = acc + xsc_ref[off:off + CONV_ROWS, :] * wsc_ref[k:k + 1, :]
        ob_ref[r0:r0 + CONV_ROWS, :] = (bg_ref[r0:r0 + CONV_ROWS, :] * acc).astype(_BF)

        acc = jnp.zeros((CONV_ROWS, width), _F32)
        for k in range(CF_K):
            off = HALO_CF + r0 + k - CF_K // 2
            acc = acc + xcf_ref[off:off + CONV_ROWS, :] * wcf_ref[k:k + 1, :]
        u = acc + bcf_ref[...]
        mu = jnp.mean(u, axis=-1, keepdims=True)
        uc = u - mu
        var = jnp.mean(uc * uc, axis=-1, keepdims=True)
        y = uc * lax.rsqrt(var + EPS) * gln_ref[...] + bln_ref[...]
        od_ref[r0:r0 + CONV_ROWS, :] = (y * jax.nn.sigmoid(y)).astype(_BF)

    mixed = jnp.concatenate([oa_ref[...], ob_ref[...], oc_ref[...], od_ref[...]], axis=1)
    o_ref[...] = h_ref[...] + mod_ref[0, 5:6, :] * _dot(mixed, wout_ref[...])


def _mix_out(h, mod, oa, oc, bg, cx, ud, w, *, tm, tiles_per_mod, mod_base, seq_tiles):
    t, d = h.shape
    halo = seq_tiles > 1
    if tiles_per_mod is None:
        mod_map = lambda i: (mod_base, 0, 0)
    else:
        mod_map = lambda i: (i // tiles_per_mod, 0, 0)
    row = lambda width: pl.BlockSpec((tm, width), lambda i: (i, 0))
    in_specs = [row(d), pl.BlockSpec((1, N_MOD, d), mod_map), row(256), row(256), row(256), row(256), row(256)]
    args = [h, mod, oa, oc, bg, cx, ud]
    if halo:
        nsc = tm // HALO_SC
        ncf = tm // HALO_CF
        last_sc = t // HALO_SC - 1
        last_cf = t // HALO_CF - 1
        in_specs += [
            pl.BlockSpec((HALO_SC, 256), lambda i: (jnp.maximum(i * nsc - 1, 0), 0)),
            pl.BlockSpec((HALO_SC, 256), lambda i: (jnp.minimum((i + 1) * nsc, last_sc), 0)),
            pl.BlockSpec((HALO_CF, 256), lambda i: (jnp.maximum(i * ncf - 1, 0), 0)),
            pl.BlockSpec((HALO_CF, 256), lambda i: (jnp.minimum((i + 1) * ncf, last_cf), 0)),
        ]
        args += [cx, cx, ud, ud]
    in_specs += [_const_spec((SC_K, 256)), _const_spec((CF_K, 256)), _const_spec((1, 256)),
                 _const_spec((1, 256)), _const_spec((1, 256)), _const_spec((d, d))]
    args += [w["w_sc"], w["w_cf"], w["b_cf"], w["g_ln"], w["b_ln"], w["w_out"]]
    return pl.pallas_call(
        functools.partial(_mix_out_kernel, halo=halo, seq_tiles=seq_tiles),
        out_shape=jax.ShapeDtypeStruct((t, d), _F32),
        grid=(t // tm,),
        in_specs=in_specs,
        out_specs=row(d),
        scratch_shapes=[pltpu.VMEM((tm + 2 * HALO_SC, 256), _F32),
                        pltpu.VMEM((tm + 2 * HALO_CF, 256), _F32),
                        pltpu.VMEM((tm, 256), _BF),
                        pltpu.VMEM((tm, 256), _BF)],
        compiler_params=_params(1),
        name="mix_out",
    )(*args)


def _rot_partner(d_rot):
    m = d_rot // 4
    r = np.arange(d_rot)
    low = (r % (2 * m)) < m
    return np.where(low, r + m, r - m), np.where(low, -1.0, 1.0)


def _rope_table(seq, d_rot, lanes_before):
    m = d_rot // 4
    d_ax = d_rot // 2
    inv = ROPE_BASE ** (-jnp.arange(0, d_ax, 2, dtype=_F32) / d_ax)
    tok = jnp.arange(seq)
    row = (tok // GRID_W).astype(_F32)
    col = (tok % GRID_W).astype(_F32)
    ar = row[:, None] * inv[None, :]
    ac = col[:, None] * inv[None, :]
    cos = jnp.concatenate([jnp.cos(ar), jnp.cos(ar), jnp.cos(ac), jnp.cos(ac)], axis=1)
    sin = jnp.concatenate([-jnp.sin(ar), jnp.sin(ar), -jnp.sin(ac), jnp.sin(ac)], axis=1)
    reps = -(-(LANES - lanes_before) // d_rot) if lanes_before == 0 else 1
    cos = jnp.concatenate([cos] * reps, axis=1)
    sin = jnp.concatenate([sin] * reps, axis=1)
    pad = LANES - lanes_before - cos.shape[1]
    c_tab = jnp.concatenate([jnp.ones((seq, lanes_before), _F32), cos, jnp.ones((seq, pad), _F32)], axis=1)
    s_tab = jnp.concatenate([jnp.zeros((seq, lanes_before), _F32), sin, jnp.zeros((seq, pad), _F32)], axis=1)
    return c_tab, s_tab


def _gather_cols(w, idx, valid, scale=None):
    out = jnp.where(jnp.asarray(valid)[None, :], jnp.take(w, jnp.asarray(idx), axis=1), 0.0)
    if scale is not None:
        out = out * jnp.asarray(scale, _F32)[None, :]
    return out


def _pack_layer(p, l):
    offs = np.concatenate([[0], np.cumsum(IN_SIZES)])
    o_cq, o_ckv, o_kr, o_sc, o_wq, o_wkv, o_cf = (int(v) for v in offs[:-1])
    scale_a = float((MLA_NOPE + MLA_ROPE) ** -0.5 * LOG2E)
    scale_w = float(WA_HEAD_DIM ** -0.5 * LOG2E)
    pa, _ = _rot_partner(MLA_ROPE)
    pw, _ = _rot_partner(WA_HEAD_DIM)

    idx = np.zeros(N_PACK, np.int64)
    valid = np.zeros(N_PACK, bool)
    scale = np.ones(N_PACK, np.float32)

    def put(dst, src, sc=1.0):
        idx[dst] = src
        valid[dst] = True
        scale[dst] = sc

    put(SEG_CQ + np.arange(MLA_Q_RANK), o_cq + np.arange(MLA_Q_RANK))
    put(SEG_KR + MLA_NOPE + np.arange(MLA_ROPE), o_kr + np.arange(MLA_ROPE))
    put(SEG_KR + LANES + MLA_NOPE + np.arange(MLA_ROPE), o_kr + pa)
    put(SEG_SC + np.arange(3 * SC_WIDTH), o_sc + np.arange(3 * SC_WIDTH))
    nq = WA_HEADS * WA_HEAD_DIM
    head_of = np.arange(nq) // WA_HEAD_DIM
    put(SEG_WQ + np.arange(nq), o_wq + np.arange(nq), scale_w)
    put(SEG_WQ + nq + np.arange(nq), o_wq + head_of * WA_HEAD_DIM + pw[np.arange(nq) % WA_HEAD_DIM], scale_w)
    put(SEG_CF + np.arange(2 * CF_WIDTH), o_cf + np.arange(2 * CF_WIDTH))
    nk = WA_KV_HEADS * WA_HEAD_DIM
    khead = np.arange(nk) // WA_HEAD_DIM
    put(SEG_WK + np.arange(nk), o_wkv + np.arange(nk))
    put(SEG_WK + nk + np.arange(nk), o_wkv + khead * WA_HEAD_DIM + pw[np.arange(nk) % WA_HEAD_DIM])
    put(SEG_WV + np.arange(nk), o_wkv + nk + np.arange(nk))
    put(SEG_CKV + np.arange(MLA_KV_RANK), o_ckv + np.arange(MLA_KV_RANK))
    w_in = _gather_cols(p["w_in"][l], idx, valid, scale).astype(_BF)

    dq = MLA_NOPE + MLA_ROPE
    qi = np.zeros(2 * MLA_HEADS * LANES, np.int64)
    qv = np.zeros(2 * MLA_HEADS * LANES, bool)
    for h in range(MLA_HEADS):
        qi[h * LANES + np.arange(dq)] = h * dq + np.arange(dq)
        qv[h * LANES + np.arange(dq)] = True
        dst = MLA_HEADS * LANES + h * LANES + MLA_NOPE + np.arange(MLA_ROPE)
        qi[dst] = h * dq + MLA_NOPE + pa
        qv[dst] = True
    w_uq = _gather_cols(p["w_mla_uq"][l], qi, qv) * scale_a
    w_uq = jnp.pad(w_uq, ((0, 256 - MLA_Q_RANK), (0, 0))).astype(_BF)
    g_q = jnp.pad(p["g_mla_q"][l], (0, 256 - MLA_Q_RANK)).reshape(1, 256)

    dkv = MLA_NOPE + MLA_V
    ki = np.zeros(MLA_HEADS * LANES + MLA_HEADS * MLA_V, np.int64)
    kvd = np.zeros(MLA_HEADS * LANES + MLA_HEADS * MLA_V, bool)
    for h in range(MLA_HEADS):
        ki[h * LANES + np.arange(MLA_NOPE)] = h * dkv + np.arange(MLA_NOPE)
        kvd[h * LANES + np.arange(MLA_NOPE)] = True
        dst = MLA_HEADS * LANES + h * MLA_V + np.arange(MLA_V)
        ki[dst] = h * dkv + MLA_NOPE + np.arange(MLA_V)
        kvd[dst] = True
    w_ukv = _gather_cols(p["w_mla_ukv"][l], ki, kvd).astype(_BF)

    return dict(
        w_in=w_in, w_uq=w_uq, g_q=g_q, w_ukv=w_ukv, g_kv=p["g_mla_kv"][l].reshape(1, MLA_KV_RANK),
        w_sc=p["w_sc_conv"][l], w_cf=p["w_cf_conv"][l], b_cf=p["b_cf_conv"][l].reshape(1, -1),
        g_ln=p["g_cf_ln"][l].reshape(1, -1), b_ln=p["b_cf_ln"][l].reshape(1, -1),
        w_out=p["w_out"][l].astype(_BF),
        g_ffn1=p["g_ffn1"][l].reshape(1, -1), g_mix=p["g_mix"][l].reshape(1, -1),
        g_ffn2=p["g_ffn2"][l].reshape(1, -1),
        w1=(p["w1_gate"][l].astype(_BF), p["w1_up"][l].astype(_BF), p["w1_down"][l].astype(_BF)),
        w2=(p["w2_gate"][l].astype(_BF), p["w2_up"][l].astype(_BF), p["w2_down"][l].astype(_BF)),
        sink=p["wa_sink"][l],
    )


def kernel(x, c, ctx, c_ctx, w_mod, b_mod, g_ffn1, w1_gate, w1_up, w1_down, g_mix, w_in, g_mla_q, w_mla_uq, g_mla_kv, w_mla_ukv, w_sc_conv, wa_sink, w_cf_conv, b_cf_conv, g_cf_ln, b_cf_ln, w_out, g_ffn2, w2_gate, w2_up, w2_down, g_final):
    b, s, d = x.shape
    n_ctx = ctx.shape[1]
    depth = w_mod.shape[0]
    p = dict(w_in=w_in, g_mla_q=g_mla_q, w_mla_uq=w_mla_uq, g_mla_kv=g_mla_kv, w_mla_ukv=w_mla_ukv,
             w_sc_conv=w_sc_conv, wa_sink=wa_sink, w_cf_conv=w_cf_conv, b_cf_conv=b_cf_conv,
             g_cf_ln=g_cf_ln, b_cf_ln=b_cf_ln, w_out=w_out, g_ffn1=g_ffn1, g_mix=g_mix, g_ffn2=g_ffn2,
             w1_gate=w1_gate, w1_up=w1_up, w1_down=w1_down, w2_gate=w2_gate, w2_up=w2_up, w2_down=w2_down)

    tm = TM_LAT
    seq_tiles = s // tm
    tabs = _rope_table(s, MLA_ROPE, MLA_NOPE) + _rope_table(s, WA_HEAD_DIM, 0)

    mod_rows = 8
    cs = jnp.concatenate([c, c_ctx[None, :], jnp.zeros((mod_rows - b - 1, d), _F32)], axis=0)
    mod_all = _modulation(cs, w_mod, b_mod).reshape(depth, mod_rows, N_MOD, d)

    lat = dict(tm=tm, tiles_per_mod=seq_tiles, mod_base=0)
    cxt = dict(tm=n_ctx, tiles_per_mod=None, mod_base=b)

    h_lat = x.reshape(b * s, d)
    h_ctx = ctx.reshape(b * n_ctx, d)
    for l in range(depth):
        last = l == depth - 1
        w = _pack_layer(p, l)
        mod = mod_all[l]

        h_lat = _ffn(h_lat, mod, w["g_ffn1"], *w["w1"], which=0, **lat)
        h_ctx = _ffn(h_ctx, mod, w["g_ffn1"], *w["w1"], which=0, **cxt)

        qa, ka, vat, bg, cx, qw, kw, vwt, ud = _in_proj(h_lat, mod, w["g_mix"], w, tabs, seq_tiles=seq_tiles, **lat)
        qa_c, ka_c, vat_c, bg_c, cx_c, qw_c, kw_c, vwt_c, ud_c = _in_proj(
            h_ctx, mod, w["g_mix"], w, None, seq_tiles=1, **cxt)

        oa = _mla_attn(qa, ka, vat, ka_c, vat_c, batch=b, tq=TQ_MLA)
        oc = _wa_attn(w["sink"], qw, kw, vwt, kw_c, vwt_c, batch=b, tq=TQ_WA)
        h_lat = _mix_out(h_lat, mod, oa, oc, bg, cx, ud, w, seq_tiles=seq_tiles, **lat)

        if not last:
            oa_c = _mla_attn(qa_c, None, None, ka_c, vat_c, batch=b, tq=n_ctx)
            oc_c = _wa_attn(w["sink"], qw_c, None, None, kw_c, vwt_c, batch=b, tq=n_ctx)
            h_ctx = _mix_out(h_ctx, mod, oa_c, oc_c, bg_c, cx_c, ud_c, w, seq_tiles=1, **cxt)
            h_ctx = _ffn(h_ctx, mod, w["g_ffn2"], *w["w2"], which=2, **cxt)

        h_lat = _ffn(h_lat, mod, w["g_ffn2"], *w["w2"], which=2,
                     g_final=g_final.reshape(1, d) if last else None, **lat)

    return h_lat.reshape(b, s, d)
```

```python
import functools

import numpy as np
import jax
import jax.numpy as jnp
from jax import lax
from jax.experimental import pallas as pl
from jax.experimental.pallas import tpu as pltpu

GRID_W = 64
ROPE_BASE = 10000.0
EPS = 1e-6
NEG = -1e30
N_MOD = 9
LOG2E = 1.4426950408889634

MLA_HEADS = 4
MLA_NOPE = 64
MLA_ROPE = 32
MLA_V = 64
MLA_Q_RANK = 192
MLA_KV_RANK = 128
SC_WIDTH = 256
SC_K = 3
WA_HEADS = 4
WA_KV_HEADS = 2
WA_HEAD_DIM = 64
WINDOW = 128
CF_WIDTH = 256
CF_K = 31
IN_SIZES = (MLA_Q_RANK, MLA_KV_RANK, MLA_ROPE, 3 * SC_WIDTH,
            WA_HEADS * WA_HEAD_DIM, 2 * WA_KV_HEADS * WA_HEAD_DIM, 2 * CF_WIDTH)

LANES = 128
SUBLANES = 8
MXU_N = 256
SM_ROWS = 32
ONES_ROWS = 16

_BF = jnp.bfloat16
_F32 = jnp.float32

SEG_CQ = 0
SEG_KR = 256
SEG_SC = 512
SEG_WQ = 1280
SEG_CF = 1792
SEG_WK = 2304
SEG_WV = 2560
SEG_CKV = 2688
N_PACK = 2816

TM_LAT = 512
TQ_MLA = 256
TK_MLA = 2048
TQ_WA = 256
FF_CHUNK = 256
VMEM_LIMIT = 52 * 1024 * 1024


def _params(n_axes, flags=None):
    return pltpu.CompilerParams(dimension_semantics=("arbitrary",) * n_axes,
                                vmem_limit_bytes=VMEM_LIMIT, flags=flags)


def _const_spec(shape):
    nd = len(shape)
    return pl.BlockSpec(shape, lambda *_: (0,) * nd, pipeline_mode=pl.Buffered(1))


def _dot(a, b):
    return jnp.dot(a, b, preferred_element_type=_F32)


def _dot_nt(a, b):
    return lax.dot_general(a, b, (((1,), (1,)), ((), ())), preferred_element_type=_F32)


def _norm_mod(x, g, shift, scale):
    y = x * lax.rsqrt(jnp.mean(x * x, axis=-1, keepdims=True) + EPS) * g
    return y * (1.0 + scale) + shift


def _mod_kernel(c_ref, w_ref, b_ref, o_ref):
    c = c_ref[...]
    a = c * jax.nn.sigmoid(c)
    o_ref[0] = jnp.dot(a, w_ref[0], preferred_element_type=_F32,
                       precision=lax.Precision.HIGHEST) + b_ref[0]


def _modulation(cs, w_mod, b_mod):
    depth, d, n = w_mod.shape
    tn = 1024
    rows = cs.shape[0]
    return pl.pallas_call(
        _mod_kernel,
        out_shape=jax.ShapeDtypeStruct((depth, rows, n), _F32),
        grid=(depth, n // tn),
        in_specs=[pl.BlockSpec((rows, d), lambda l, j: (0, 0)),
                  pl.BlockSpec((1, d, tn), lambda l, j: (l, 0, j)),
                  pl.BlockSpec((1, 1, tn), lambda l, j: (l, 0, j))],
        out_specs=pl.BlockSpec((1, rows, tn), lambda l, j: (l, 0, j)),
        compiler_params=_params(2),
        name="modulation",
    )(cs, w_mod, b_mod.reshape(depth, 1, n))


def _ffn_kernel(h_ref, mod_ref, g_ref, wg_ref, wu_ref, wd_ref, *rest, which, final):
    if final:
        gf_ref, o_ref, a_ref = rest
    else:
        o_ref, a_ref = rest
    x = h_ref[...]
    shift = mod_ref[0, 3 * which:3 * which + 1, :]
    scale = mod_ref[0, 3 * which + 1:3 * which + 2, :]
    gate = mod_ref[0, 3 * which + 2:3 * which + 3, :]
    xb = _norm_mod(x, g_ref[...], shift, scale).astype(_BF)
    d_ff = wg_ref.shape[1]
    for j in range(d_ff // FF_CHUNK):
        sl = slice(j * FF_CHUNK, (j + 1) * FF_CHUNK)
        gt = _dot(xb, wg_ref[:, sl])
        up = _dot(xb, wu_ref[:, sl])
        a_ref[:, sl] = (gt * jax.nn.sigmoid(gt) * up).astype(_BF)
    y = x + 0.5 * gate * _dot(a_ref[...], wd_ref[...])
    if final:
        y = y * lax.rsqrt(jnp.mean(y * y, axis=-1, keepdims=True) + EPS) * gf_ref[...]
    o_ref[...] = y


def _ffn(h, mod, g, wg, wu, wd, *, which, tm, tiles_per_mod, mod_base, g_final=None):
    t, d = h.shape
    d_ff = wg.shape[1]
    final = g_final is not None
    if tiles_per_mod is None:
        mod_map = lambda i: (mod_base, 0, 0)
    else:
        mod_map = lambda i: (i // tiles_per_mod, 0, 0)
    in_specs = [pl.BlockSpec((tm, d), lambda i: (i, 0)),
                pl.BlockSpec((1, N_MOD, d), mod_map),
                _const_spec((1, d)),
                _const_spec((d, d_ff)), _const_spec((d, d_ff)), _const_spec((d_ff, d))]
    args = [h, mod, g, wg, wu, wd]
    if final:
        in_specs.append(_const_spec((1, d)))
        args.append(g_final)
    return pl.pallas_call(
        functools.partial(_ffn_kernel, which=which, final=final),
        out_shape=jax.ShapeDtypeStruct((t, d), _F32),
        grid=(t // tm,),
        in_specs=in_specs,
        out_specs=pl.BlockSpec((tm, d), lambda i: (i, 0)),
        scratch_shapes=[pltpu.VMEM((tm, d_ff), _BF)],
        compiler_params=_params(1),
        name="ffn",
    )(*args)


def _in_proj_kernel(h_ref, mod_ref, g_ref, win_ref, gq_ref, wuq_ref, gkv_ref, wukv_ref, *rest, rope):
    if rope:
        ca_ref, sa_ref, cw_ref, sw_ref = rest[:4]
        rest = rest[4:]
    qa_ref, ka_ref, vat_ref, bg_ref, cx_ref, qw_ref, kw_ref, vwt_ref, ud_ref = rest

    x = h_ref[...]
    xb = _norm_mod(x, g_ref[...], mod_ref[0, 3:4, :], mod_ref[0, 4:5, :]).astype(_BF)

    def seg(lo, width):
        return _dot(xb, win_ref[:, lo:lo + width])

    cq = seg(SEG_CQ, 256)
    cqn = cq * lax.rsqrt(jnp.sum(cq * cq, axis=-1, keepdims=True) * (1.0 / MLA_Q_RANK) + EPS) * gq_ref[...]
    qq = _dot(cqn.astype(_BF), wuq_ref[...])
    q = qq[:, :MLA_HEADS * LANES]
    if rope:
        ca = ca_ref[...]
        sa = sa_ref[...]
        q = (q * jnp.concatenate([ca] * MLA_HEADS, axis=1)
             + qq[:, MLA_HEADS * LANES:] * jnp.concatenate([sa] * MLA_HEADS, axis=1))
    qa_ref[...] = q.astype(_BF)

    ckv = seg(SEG_CKV, 128)
    ckvn = ckv * lax.rsqrt(jnp.mean(ckv * ckv, axis=-1, keepdims=True) + EPS) * gkv_ref[...]
    kv = _dot(ckvn.astype(_BF), wukv_ref[...])
    kr2 = seg(SEG_KR, 256)
    kr = kr2[:, :LANES]
    if rope:
        kr = kr * ca + kr2[:, LANES:] * sa
    ka_ref[...] = (kv[:, :MLA_HEADS * LANES] + jnp.concatenate([kr] * MLA_HEADS, axis=1)).astype(_BF)
    vat_ref[...] = kv[:, MLA_HEADS * LANES:].T.astype(_BF)

    sc = seg(SEG_SC, 768)
    bg_ref[...] = sc[:, :256]
    cx_ref[...] = sc[:, 256:512] * sc[:, 512:]

    wq2 = seg(SEG_WQ, 512)
    qw = wq2[:, :256]
    wk2 = seg(SEG_WK, 256)
    kw = wk2[:, :LANES]
    if rope:
        cw = cw_ref[...]
        sw = sw_ref[...]
        qw = qw * jnp.concatenate([cw, cw], axis=1) + wq2[:, 256:] * jnp.concatenate([sw, sw], axis=1)
        kw = kw * cw + wk2[:, LANES:] * sw
    qw_ref[...] = qw.astype(_BF)
    lane = lax.broadcasted_iota(jnp.int32, kw.shape, 1)
    low = lane < WA_HEAD_DIM
    kw_r = pltpu.roll(kw, WA_HEAD_DIM, axis=1)
    zero = jnp.zeros_like(kw)
    kw_ref[...] = jnp.concatenate(
        [jnp.where(low, kw, zero), jnp.where(low, zero, kw_r),
         jnp.where(low, kw_r, zero), jnp.where(low, zero, kw)], axis=1).astype(_BF)
    vwt_ref[...] = seg(SEG_WV, 128).T.astype(_BF)

    cf = seg(SEG_CF, 512)
    ud_ref[...] = cf[:, :256] * jax.nn.sigmoid(cf[:, 256:])


def _in_proj(h, mod, g, w, tabs, *, tm, tiles_per_mod, mod_base, seq_tiles):
    t, d = h.shape
    rope = tabs is not None
    if tiles_per_mod is None:
        mod_map = lambda i: (mod_base, 0, 0)
    else:
        mod_map = lambda i: (i // tiles_per_mod, 0, 0)
    in_specs = [pl.BlockSpec((tm, d), lambda i: (i, 0)),
                pl.BlockSpec((1, N_MOD, d), mod_map),
                _const_spec((1, d)),
                _const_spec((d, N_PACK)),
                _const_spec((1, 256)), _const_spec((256, 1024)),
                _const_spec((1, 128)), _const_spec((128, 768))]
    args = [h, mod, g, w["w_in"], w["g_q"], w["w_uq"], w["g_kv"], w["w_ukv"]]
    if rope:
        in_specs += [pl.BlockSpec((tm, LANES), lambda i: (i % seq_tiles, 0))] * 4
        args += list(tabs)
    row = lambda width: pl.BlockSpec((tm, width), lambda i: (i, 0))
    col = lambda height: pl.BlockSpec((height, tm), lambda i: (0, i))
    out_shape = (
        jax.ShapeDtypeStruct((t, 512), _BF),
        jax.ShapeDtypeStruct((t, 512), _BF),
        jax.ShapeDtypeStruct((256, t), _BF),
        jax.ShapeDtypeStruct((t, 256), _F32),
        jax.ShapeDtypeStruct((t, 256), _F32),
        jax.ShapeDtypeStruct((t, 256), _BF),
        jax.ShapeDtypeStruct((t, 512), _BF),
        jax.ShapeDtypeStruct((128, t), _BF),
        jax.ShapeDtypeStruct((t, 256), _F32),
    )
    out_specs = (row(512), row(512), col(256), row(256), row(256), row(256), row(512), col(128), row(256))
    return pl.pallas_call(
        functools.partial(_in_proj_kernel, rope=rope),
        out_shape=out_shape,
        grid=(t // tm,),
        in_specs=in_specs,
        out_specs=out_specs,
        compiler_params=_params(1),
        name="in_proj",
    )(*args)


def _mla_kernel(q_ref, *refs, has_lat, n_lat_tiles):
    if has_lat:
        kl_ref, vl_ref, kc_ref, vc_ref, o_ref, s_ref, p_ref = refs
    else:
        kc_ref, vc_ref, o_ref, s_ref, p_ref = refs
    tq = q_ref.shape[0]
    n_ctx = kc_ref.shape[0]

    def softmax_stage(tk, tile_max, carry):
        stats = []
        for h in range(MLA_HEADS):
            m = carry[h][0]
            m_new = jnp.maximum(m, tile_max[h])
            stats.append((m_new, jnp.exp2(m - m_new)))
            for r0 in range(0, tk, SM_ROWS):
                p = jnp.exp2(s_ref[h, r0:r0 + SM_ROWS, :] - m_new)
                p_ref[h, r0:r0 + SM_ROWS, :] = p.astype(_BF)
        return stats

    def scores_stage(k_ref, key_slice, tk):
        tile_max = []
        for h in range(MLA_HEADS):
            s = _dot_nt(k_ref[key_slice, h * LANES:(h + 1) * LANES],
                        q_ref[:, h * LANES:(h + 1) * LANES])
            s_ref[h, 0:tk, :] = s
            tile_max.append(jnp.max(s, axis=0, keepdims=True))
        return tuple(tile_max)

    def value_stage(vt_ref, key_slice, tk, stats, carry):
        ones = jnp.ones((ONES_ROWS, tk), _BF)
        new = []
        for h in range(MLA_HEADS):
            m_new, alpha = stats[h]
            v1 = jnp.concatenate([vt_ref[h * MLA_V:(h + 1) * MLA_V, key_slice], ones], axis=0)
            new.append((m_new, alpha * carry[h][1] + _dot(v1, p_ref[h, 0:tk, :])))
        return tuple(new)

    carry = tuple((jnp.full((1, tq), NEG, _F32), jnp.zeros((MLA_V + ONES_ROWS, tq), _F32))
                  for _ in range(MLA_HEADS))
    if has_lat:
        def lat(t):
            return pl.ds(pl.multiple_of(t * TK_MLA, TK_MLA), TK_MLA)

        def body(t, state, last):
            tile_max, carry = state
            stats = softmax_stage(TK_MLA, tile_max, carry)
            if last:
                tile_max = scores_stage(kc_ref, slice(None), n_ctx)
            else:
                tile_max = scores_stage(kl_ref, lat(t + 1), TK_MLA)
            return tile_max, value_stage(vl_ref, lat(t), TK_MLA, stats, carry)

        state = (scores_stage(kl_ref, lat(0), TK_MLA), carry)
        state = lax.fori_loop(0, n_lat_tiles - 1, functools.partial(body, last=False), state)
        tile_max, carry = body(n_lat_tiles - 1, state, True)
    else:
        tile_max = scores_stage(kc_ref, slice(None), n_ctx)
    stats = softmax_stage(n_ctx, tile_max, carry)
    carry = value_stage(vc_ref, slice(None), n_ctx, stats, carry)
    outs = [acc[:MLA_V] / acc[MLA_V:MLA_V + 1] for (_, acc) in carry]
    o_ref[...] = jnp.concatenate(outs, axis=0).T.astype(_BF)


def _mla_attn(q, k_lat, vt_lat, k_ctx, vt_ctx, *, batch, tq):
    t = q.shape[0]
    per_b = t // batch
    nq = per_b // tq
    c = k_ctx.shape[0] // batch
    has_lat = k_lat is not None
    in_specs = [pl.BlockSpec((tq, 512), lambda b, j: (b * nq + j, 0))]
    args = [q]
    n_lat_tiles = 0
    if has_lat:
        s = k_lat.shape[0] // batch
        n_lat_tiles = s // TK_MLA
        in_specs += [pl.BlockSpec((s, 512), lambda b, j: (b, 0)),
                     pl.BlockSpec((256, s), lambda b, j: (0, b))]
        args += [k_lat, vt_lat]
    in_specs += [pl.BlockSpec((c, 512), lambda b, j: (b, 0)),
                 pl.BlockSpec((256, c), lambda b, j: (0, b))]
    args += [k_ctx, vt_ctx]
    return pl.pallas_call(
        functools.partial(_mla_kernel, has_lat=has_lat, n_lat_tiles=n_lat_tiles),
        out_shape=jax.ShapeDtypeStruct((t, 256), _BF),
        grid=(batch, nq),
        in_specs=in_specs,
        out_specs=pl.BlockSpec((tq, 256), lambda b, j: (b * nq + j, 0)),
        scratch_shapes=[pltpu.VMEM((MLA_HEADS, max(TK_MLA, c), tq), _F32),
                        pltpu.VMEM((MLA_HEADS, max(TK_MLA, c), tq), _BF)],
        compiler_params=_params(2),
        name="mla_attn",
    )(*args)


def _wa_kernel(sink_ref, q_ref, *refs, has_lat, seq_len):
    if has_lat:
        kl_ref, vl_ref, kc_ref, vc_ref, o_ref = refs
    else:
        kc_ref, vc_ref, o_ref = refs
    tq = q_ref.shape[0]
    win_keys = tq + 2 * WINDOW
    if has_lat:
        q0 = pl.program_id(1) * tq
        start = pl.multiple_of(jnp.clip(q0 - WINDOW, 0, seq_len - win_keys), LANES)
        kpos = start + lax.broadcasted_iota(jnp.int32, (win_keys, tq), 0)
        qpos = q0 + lax.broadcasted_iota(jnp.int32, (win_keys, tq), 1)
        valid = jnp.abs(kpos - qpos) <= WINDOW
    outs = []
    for hq in range(WA_HEADS):
        g = hq // (WA_HEADS // WA_KV_HEADS)
        lanes = slice(hq * LANES, (hq + 1) * LANES)
        rows = slice(g * WA_HEAD_DIM, (g + 1) * WA_HEAD_DIM)
        qpair = q_ref[:, g * LANES:(g + 1) * LANES]
        snk = sink_ref[hq] * LOG2E
        s_ctx = _dot_nt(kc_ref[:, lanes], qpair)
        m = jnp.maximum(jnp.max(s_ctx, axis=0, keepdims=True), snk)
        if has_lat:
            s_loc = jnp.where(valid, _dot_nt(kl_ref[pl.ds(start, win_keys), lanes], qpair), NEG)
            m = jnp.maximum(m, jnp.max(s_loc, axis=0, keepdims=True))
        p_ctx = jnp.exp2(s_ctx - m)
        l = jnp.sum(p_ctx, axis=0, keepdims=True) + jnp.exp2(snk - m)
        o_t = _dot(vc_ref[rows, :], p_ctx.astype(_BF))
        if has_lat:
            p_loc = jnp.exp2(s_loc - m)
            l = l + jnp.sum(p_loc, axis=0, keepdims=True)
            o_t = o_t + _dot(vl_ref[rows, pl.ds(start, win_keys)], p_loc.astype(_BF))
        outs.append(o_t / l)
    o_ref[...] = jnp.concatenate(outs, axis=0).T.astype(_BF)


def _wa_attn(sink, q, k_lat, vt_lat, k_ctx, vt_ctx, *, batch, tq):
    t = q.shape[0]
    per_b = t // batch
    nq = per_b // tq
    c = k_ctx.shape[0] // batch
    has_lat = k_lat is not None
    in_specs = [pl.BlockSpec(memory_space=pltpu.SMEM),
                pl.BlockSpec((tq, 256), lambda b, j: (b * nq + j, 0))]
    args = [sink, q]
    seq_len = 0
    if has_lat:
        seq_len = k_lat.shape[0] // batch
        in_specs += [pl.BlockSpec((seq_len, 512), lambda b, j: (b, 0)),
                     pl.BlockSpec((128, seq_len), lambda b, j: (0, b))]
        args += [k_lat, vt_lat]
    in_specs += [pl.BlockSpec((c, 512), lambda b, j: (b, 0)),
                 pl.BlockSpec((128, c), lambda b, j: (0, b))]
    args += [k_ctx, vt_ctx]
    return pl.pallas_call(
        functools.partial(_wa_kernel, has_lat=has_lat, seq_len=seq_len),
        out_shape=jax.ShapeDtypeStruct((t, 256), _BF),
        grid=(batch, nq),
        in_specs=in_specs,
        out_specs=pl.BlockSpec((tq, 256), lambda b, j: (b * nq + j, 0)),
        compiler_params=_params(2),
        name="wa_attn",
    )(*args)


HALO_SC = 8
HALO_CF = 16
CONV_ROWS = 64


def _mix_out_kernel(h_ref, mod_ref, oa_ref, oc_ref, bg_ref, cx_ref, ud_ref, *rest, halo, seq_tiles):
    if halo:
        cxp_ref, cxn_ref, udp_ref, udn_ref = rest[:4]
        rest = rest[4:]
    wsc_ref, wcf_ref, bcf_ref, gln_ref, bln_ref, wout_ref, o_ref, xsc_ref, xcf_ref, ob_ref, od_ref = rest
    tm = h_ref.shape[0]
    width = cx_ref.shape[1]

    if halo:
        i = pl.program_id(0) % seq_tiles
        has_prev = (i != 0).astype(_F32)
        has_next = (i != seq_tiles - 1).astype(_F32)
        xsc_ref[0:HALO_SC, :] = cxp_ref[...] * has_prev
        xsc_ref[HALO_SC + tm:, :] = cxn_ref[...] * has_next
        xcf_ref[0:HALO_CF, :] = udp_ref[...] * has_prev
        xcf_ref[HALO_CF + tm:, :] = udn_ref[...] * has_next
    else:
        xsc_ref[0:HALO_SC, :] = jnp.zeros((HALO_SC, width), _F32)
        xsc_ref[HALO_SC + tm:, :] = jnp.zeros((HALO_SC, width), _F32)
        xcf_ref[0:HALO_CF, :] = jnp.zeros((HALO_CF, width), _F32)
        xcf_ref[HALO_CF + tm:, :] = jnp.zeros((HALO_CF, width), _F32)
    xsc_ref[HALO_SC:HALO_SC + tm, :] = cx_ref[...]
    xcf_ref[HALO_CF:HALO_CF + tm, :] = ud_ref[...]

    for r0 in range(0, tm, CONV_ROWS):
        acc = jnp.zeros((CONV_ROWS, width), _F32)
        for k in range(SC_K):
            off = HALO_SC + r0 + k - SC_K // 2
            acc = acc + xsc_ref[off:off + CONV_ROWS, :] * wsc_ref[k:k + 1, :]
        ob_ref[r0:r0 + CONV_ROWS, :] = (bg_ref[r0:r0 + CONV_ROWS, :] * acc).astype(_BF)

        acc = jnp.zeros((CONV_ROWS, width), _F32)
        for k in range(CF_K):
            off = HALO_CF + r0 + k - CF_K // 2
            acc = acc + xcf_ref[off:off + CONV_ROWS, :] * wcf_ref[k:k + 1, :]
        u = acc + bcf_ref[...]
        mu = jnp.mean(u, axis=-1, keepdims=True)
        uc = u - mu
        var = jnp.mean(uc * uc, axis=-1, keepdims=True)
        y = uc * lax.rsqrt(var + EPS) * gln_ref[...] + bln_ref[...]
        od_ref[r0:r0 + CONV_ROWS, :] = (y * jax.nn.sigmoid(y)).astype(_BF)

    mixed = jnp.concatenate([oa_ref[...], ob_ref[...], oc_ref[...], od_ref[...]], axis=1)
    o_ref[...] = h_ref[...] + mod_ref[0, 5:6, :] * _dot(mixed, wout_ref[...])


def _mix_out(h, mod, oa, oc, bg, cx, ud, w, *, tm, tiles_per_mod, mod_base, seq_tiles):
    t, d = h.shape
    halo = seq_tiles > 1
    if tiles_per_mod is None:
        mod_map = lambda i: (mod_base, 0, 0)
    else:
        mod_map = lambda i: (i // tiles_per_mod, 0, 0)
    row = lambda width: pl.BlockSpec((tm, width), lambda i: (i, 0))
    in_specs = [row(d), pl.BlockSpec((1, N_MOD, d), mod_map), row(256), row(256), row(256), row(256), row(256)]
    args = [h, mod, oa, oc, bg, cx, ud]
    if halo:
        nsc = tm // HALO_SC
        ncf = tm // HALO_CF
        last_sc = t // HALO_SC - 1
        last_cf = t // HALO_CF - 1
        in_specs += [
            pl.BlockSpec((HALO_SC, 256), lambda i: (jnp.maximum(i * nsc - 1, 0), 0)),
            pl.BlockSpec((HALO_SC, 256), lambda i: (jnp.minimum((i + 1) * nsc, last_sc), 0)),
            pl.BlockSpec((HALO_CF, 256), lambda i: (jnp.maximum(i * ncf - 1, 0), 0)),
            pl.BlockSpec((HALO_CF, 256), lambda i: (jnp.minimum((i + 1) * ncf, last_cf), 0)),
        ]
        args += [cx, cx, ud, ud]
    in_specs += [_const_spec((SC_K, 256)), _const_spec((CF_K, 256)), _const_spec((1, 256)),
                 _const_spec((1, 256)), _const_spec((1, 256)), _const_spec((d, d))]
    args += [w["w_sc"], w["w_cf"], w["b_cf"], w["g_ln"], w["b_ln"], w["w_out"]]
    return pl.pallas_call(
        functools.partial(_mix_out_kernel, halo=halo, seq_tiles=seq_tiles),
        out_shape=jax.ShapeDtypeStruct((t, d), _F32),
        grid=(t // tm,),
        in_specs=in_specs,
        out_specs=row(d),
        scratch_shapes=[pltpu.VMEM((tm + 2 * HALO_SC, 256), _F32),
                        pltpu.VMEM((tm + 2 * HALO_CF, 256), _F32),
                        pltpu.VMEM((tm, 256), _BF),
                        pltpu.VMEM((tm, 256), _BF)],
        compiler_params=_params(1),
        name="mix_out",
    )(*args)


def _rot_partner(d_rot):
    m = d_rot // 4
    r = np.arange(d_rot)
    low = (r % (2 * m)) < m
    return np.where(low, r + m, r - m), np.where(low, -1.0, 1.0)


def _rope_table(seq, d_rot, lanes_before):
    m = d_rot // 4
    d_ax = d_rot // 2
    inv = ROPE_BASE ** (-jnp.arange(0, d_ax, 2, dtype=_F32) / d_ax)
    tok = jnp.arange(seq)
    row = (tok // GRID_W).astype(_F32)
    col = (tok % GRID_W).astype(_F32)
    ar = row[:, None] * inv[None, :]
    ac = col[:, None] * inv[None, :]
    cos = jnp.concatenate([jnp.cos(ar), jnp.cos(ar), jnp.cos(ac), jnp.cos(ac)], axis=1)
    sin = jnp.concatenate([-jnp.sin(ar), jnp.sin(ar), -jnp.sin(ac), jnp.sin(ac)], axis=1)
    reps = -(-(LANES - lanes_before) // d_rot) if lanes_before == 0 else 1
    cos = jnp.concatenate([cos] * reps, axis=1)
    sin = jnp.concatenate([sin] * reps, axis=1)
    pad = LANES - lanes_before - cos.shape[1]
    c_tab = jnp.concatenate([jnp.ones((seq, lanes_before), _F32), cos, jnp.ones((seq, pad), _F32)], axis=1)
    s_tab = jnp.concatenate([jnp.zeros((seq, lanes_before), _F32), sin, jnp.zeros((seq, pad), _F32)], axis=1)
    return c_tab, s_tab


def _gather_cols(w, idx, valid, scale=None):
    out = jnp.where(jnp.asarray(valid)[None, :], jnp.take(w, jnp.asarray(idx), axis=1), 0.0)
    if scale is not None:
        out = out * jnp.asarray(scale, _F32)[None, :]
    return out


def _pack_layer(p, l):
    offs = np.concatenate([[0], np.cumsum(IN_SIZES)])
    o_cq, o_ckv, o_kr, o_sc, o_wq, o_wkv, o_cf = (int(v) for v in offs[:-1])
    scale_a = float((MLA_NOPE + MLA_ROPE) ** -0.5 * LOG2E)
    scale_w = float(WA_HEAD_DIM ** -0.5 * LOG2E)
    pa, _ = _rot_partner(MLA_ROPE)
    pw, _ = _rot_partner(WA_HEAD_DIM)

    idx = np.zeros(N_PACK, np.int64)
    valid = np.zeros(N_PACK, bool)
    scale = np.ones(N_PACK, np.float32)

    def put(dst, src, sc=1.0):
        idx[dst] = src
        valid[dst] = True
        scale[dst] = sc

    put(SEG_CQ + np.arange(MLA_Q_RANK), o_cq + np.arange(MLA_Q_RANK))
    put(SEG_KR + MLA_NOPE + np.arange(MLA_ROPE), o_kr + np.arange(MLA_ROPE))
    put(SEG_KR + LANES + MLA_NOPE + np.arange(MLA_ROPE), o_kr + pa)
    put(SEG_SC + np.arange(3 * SC_WIDTH), o_sc + np.arange(3 * SC_WIDTH))
    nq = WA_HEADS * WA_HEAD_DIM
    head_of = np.arange(nq) // WA_HEAD_DIM
    put(SEG_WQ + np.arange(nq), o_wq + np.arange(nq), scale_w)
    put(SEG_WQ + nq + np.arange(nq), o_wq + head_of * WA_HEAD_DIM + pw[np.arange(nq) % WA_HEAD_DIM], scale_w)
    put(SEG_CF + np.arange(2 * CF_WIDTH), o_cf + np.arange(2 * CF_WIDTH))
    nk = WA_KV_HEADS * WA_HEAD_DIM
    khead = np.arange(nk) // WA_HEAD_DIM
    put(SEG_WK + np.arange(nk), o_wkv + np.arange(nk))
    put(SEG_WK + nk + np.arange(nk), o_wkv + khead * WA_HEAD_DIM + pw[np.arange(nk) % WA_HEAD_DIM])
    put(SEG_WV + np.arange(nk), o_wkv + nk + np.arange(nk))
    put(SEG_CKV + np.arange(MLA_KV_RANK), o_ckv + np.arange(MLA_KV_RANK))
    w_in = _gather_cols(p["w_in"][l], idx, valid, scale).astype(_BF)

    dq = MLA_NOPE + MLA_ROPE
    qi = np.zeros(2 * MLA_HEADS * LANES, np.int64)
    qv = np.zeros(2 * MLA_HEADS * LANES, bool)
    for h in range(MLA_HEADS):
        qi[h * LANES + np.arange(dq)] = h * dq + np.arange(dq)
        qv[h * LANES + np.arange(dq)] = True
        dst = MLA_HEADS * LANES + h * LANES + MLA_NOPE + np.arange(MLA_ROPE)
        qi[dst] = h * dq + MLA_NOPE + pa
        qv[dst] = True
    w_uq = _gather_cols(p["w_mla_uq"][l], qi, qv) * scale_a
    w_uq = jnp.pad(w_uq, ((0, 256 - MLA_Q_RANK), (0, 0))).astype(_BF)
    g_q = jnp.pad(p["g_mla_q"][l], (0, 256 - MLA_Q_RANK)).reshape(1, 256)

    dkv = MLA_NOPE + MLA_V
    ki = np.zeros(MLA_HEADS * LANES + MLA_HEADS * MLA_V, np.int64)
    kvd = np.zeros(MLA_HEADS * LANES + MLA_HEADS * MLA_V, bool)
    for h in range(MLA_HEADS):
        ki[h * LANES + np.arange(MLA_NOPE)] = h * dkv + np.arange(MLA_NOPE)
        kvd[h * LANES + np.arange(MLA_NOPE)] = True
        dst = MLA_HEADS * LANES + h * MLA_V + np.arange(MLA_V)
        ki[dst] = h * dkv + MLA_NOPE + np.arange(MLA_V)
        kvd[dst] = True
    w_ukv = _gather_cols(p["w_mla_ukv"][l], ki, kvd).astype(_BF)

    return dict(
        w_in=w_in, w_uq=w_uq, g_q=g_q, w_ukv=w_ukv, g_kv=p["g_mla_kv"][l].reshape(1, MLA_KV_RANK),
        w_sc=p["w_sc_conv"][l], w_cf=p["w_cf_conv"][l], b_cf=p["b_cf_conv"][l].reshape(1, -1),
        g_ln=p["g_cf_ln"][l].reshape(1, -1), b_ln=p["b_cf_ln"][l].reshape(1, -1),
        w_out=p["w_out"][l].astype(_BF),
        g_ffn1=p["g_ffn1"][l].reshape(1, -1), g_mix=p["g_mix"][l].reshape(1, -1),
        g_ffn2=p["g_ffn2"][l].reshape(1, -1),
        w1=(p["w1_gate"][l].astype(_BF), p["w1_up"][l].astype(_BF), p["w1_down"][l].astype(_BF)),
        w2=(p["w2_gate"][l].astype(_BF), p["w2_up"][l].astype(_BF), p["w2_down"][l].astype(_BF)),
        sink=p["wa_sink"][l],
    )


def kernel(x, c, ctx, c_ctx, w_mod, b_mod, g_ffn1, w1_gate, w1_up, w1_down, g_mix, w_in, g_mla_q, w_mla_uq, g_mla_kv, w_mla_ukv, w_sc_conv, wa_sink, w_cf_conv, b_cf_conv, g_cf_ln, b_cf_ln, w_out, g_ffn2, w2_gate, w2_up, w2_down, g_final):
    b, s, d = x.shape
    n_ctx = ctx.shape[1]
    depth = w_mod.shape[0]
    p = dict(w_in=w_in, g_mla_q=g_mla_q, w_mla_uq=w_mla_uq, g_mla_kv=g_mla_kv, w_mla_ukv=w_mla_ukv,
             w_sc_conv=w_sc_conv, wa_sink=wa_sink, w_cf_conv=w_cf_conv, b_cf_conv=b_cf_conv,
             g_cf_ln=g_cf_ln, b_cf_ln=b_cf_ln, w_out=w_out, g_ffn1=g_ffn1, g_mix=g_mix, g_ffn2=g_ffn2,
             w1_gate=w1_gate, w1_up=w1_up, w1_down=w1_down, w2_gate=w2_gate, w2_up=w2_up, w2_down=w2_down)

    tm = TM_LAT
    seq_tiles = s // tm
    tabs = _rope_table(s, MLA_ROPE, MLA_NOPE) + _rope_table(s, WA_HEAD_DIM, 0)

    mod_rows = 8
    cs = jnp.concatenate([c, c_ctx[None, :], jnp.zeros((mod_rows - b - 1, d), _F32)], axis=0)
    mod_all = _modulation(cs, w_mod, b_mod).reshape(depth, mod_rows, N_MOD, d)

    lat = dict(tm=tm, tiles_per_mod=seq_tiles, mod_base=0)
    cxt = dict(tm=n_ctx, tiles_per_mod=None, mod_base=b)

    h_lat = x.reshape(b * s, d)
    h_ctx = ctx.reshape(b * n_ctx, d)
    for l in range(depth):
        last = l == depth - 1
        w = _pack_layer(p, l)
        mod = mod_all[l]

        h_lat = _ffn(h_lat, mod, w["g_ffn1"], *w["w1"], which=0, **lat)
        h_ctx = _ffn(h_ctx, mod, w["g_ffn1"], *w["w1"], which=0, **cxt)

        qa, ka, vat, bg, cx, qw, kw, vwt, ud = _in_proj(h_lat, mod, w["g_mix"], w, tabs, seq_tiles=seq_tiles, **lat)
        qa_c, ka_c, vat_c, bg_c, cx_c, qw_c, kw_c, vwt_c, ud_c = _in_proj(
            h_ctx, mod, w["g_mix"], w, None, seq_tiles=1, **cxt)

        oa = _mla_attn(qa, ka, vat, ka_c, vat_c, batch=b, tq=TQ_MLA)
        oc = _wa_attn(w["sink"], qw, kw, vwt, kw_c, vwt_c, batch=b, tq=TQ_WA)
        h_lat = _mix_out(h_lat, mod, oa, oc, bg, cx, ud, w, seq_tiles=seq_tiles, **lat)

        if not last:
            oa_c = _mla_attn(qa_c, None, None, ka_c, vat_c, batch=b, tq=n_ctx)
            oc_c = _wa_attn(w["sink"], qw_c, None, None, kw_c, vwt_c, batch=b, tq=n_ctx)
            h_ctx = _mix_out(h_ctx, mod, oa_c, oc_c, bg_c, cx_c, ud_c, w, seq_tiles=1, **cxt)
            h_ctx = _ffn(h_ctx, mod, w["g_ffn2"], *w["w2"], which=2, **cxt)

        h_lat = _ffn(h_lat, mod, w["g_ffn2"], *w["w2"], which=2,
                     g_final=g_final.reshape(1, d) if last else None, **lat)

    return h_lat.reshape(b, s, d)
```

```python
import functools

import numpy as np
import jax
import jax.numpy as jnp
from jax import lax
from jax.experimental import pallas as pl
from jax.experimental.pallas import tpu as pltpu

GRID_W = 64
ROPE_BASE = 10000.0
EPS = 1e-6
NEG = -1e30
N_MOD = 9
LOG2E = 1.4426950408889634

MLA_HEADS = 4
MLA_NOPE = 64
MLA_ROPE = 32
MLA_V = 64
MLA_Q_RANK = 192
MLA_KV_RANK = 128
SC_WIDTH = 256
SC_K = 3
WA_HEADS = 4
WA_KV_HEADS = 2
WA_HEAD_DIM = 64
WINDOW = 128
CF_WIDTH = 256
CF_K = 31
IN_SIZES = (MLA_Q_RANK, MLA_KV_RANK, MLA_ROPE, 3 * SC_WIDTH,
            WA_HEADS * WA_HEAD_DIM, 2 * WA_KV_HEADS * WA_HEAD_DIM, 2 * CF_WIDTH)

LANES = 128
SUBLANES = 8
MXU_N = 256
SM_ROWS = 32
ONES_ROWS = 16

_BF = jnp.bfloat16
_F32 = jnp.float32

SEG_CQ = 0
SEG_KR = 256
SEG_SC = 512
SEG_WQ = 1280
SEG_CF = 1792
SEG_WK = 2304
SEG_WV = 2560
SEG_CKV = 2688
N_PACK = 2816

TM_LAT = 512
TQ_MLA = 256
TK_MLA = 2048
TQ_WA = 256
FF_CHUNK = 256
VMEM_LIMIT = 52 * 1024 * 1024


def _params(n_axes, flags=None):
    return pltpu.CompilerParams(dimension_semantics=("arbitrary",) * n_axes,
                                vmem_limit_bytes=VMEM_LIMIT, flags=flags)


def _const_spec(shape):
    nd = len(shape)
    return pl.BlockSpec(shape, lambda *_: (0,) * nd, pipeline_mode=pl.Buffered(1))


def _dot(a, b):
    return jnp.dot(a, b, preferred_element_type=_F32)


def _dot_nt(a, b):
    return lax.dot_general(a, b, (((1,), (1,)), ((), ())), preferred_element_type=_F32)


def _norm_mod(x, g, shift, scale):
    y = x * lax.rsqrt(jnp.mean(x * x, axis=-1, keepdims=True) + EPS) * g
    return y * (1.0 + scale) + shift


def _mod_kernel(c_ref, w_ref, b_ref, o_ref):
    c = c_ref[...]
    a = c * jax.nn.sigmoid(c)
    o_ref[0] = jnp.dot(a, w_ref[0], preferred_element_type=_F32,
                       precision=lax.Precision.HIGHEST) + b_ref[0]


def _modulation(cs, w_mod, b_mod):
    depth, d, n = w_mod.shape
    tn = 1024
    rows = cs.shape[0]
    return pl.pallas_call(
        _mod_kernel,
        out_shape=jax.ShapeDtypeStruct((depth, rows, n), _F32),
        grid=(depth, n // tn),
        in_specs=[pl.BlockSpec((rows, d), lambda l, j: (0, 0)),
                  pl.BlockSpec((1, d, tn), lambda l, j: (l, 0, j)),
                  pl.BlockSpec((1, 1, tn), lambda l, j: (l, 0, j))],
        out_specs=pl.BlockSpec((1, rows, tn), lambda l, j: (l, 0, j)),
        compiler_params=_params(2),
        name="modulation",
    )(cs, w_mod, b_mod.reshape(depth, 1, n))


def _ffn_kernel(h_ref, mod_ref, g_ref, wg_ref, wu_ref, wd_ref, *rest, which, final):
    if final:
        gf_ref, o_ref, a_ref = rest
    else:
        o_ref, a_ref = rest
    x = h_ref[...]
    shift = mod_ref[0, 3 * which:3 * which + 1, :]
    scale = mod_ref[0, 3 * which + 1:3 * which + 2, :]
    gate = mod_ref[0, 3 * which + 2:3 * which + 3, :]
    xb = _norm_mod(x, g_ref[...], shift, scale).astype(_BF)
    d_ff = wg_ref.shape[1]
    for j in range(d_ff // FF_CHUNK):
        sl = slice(j * FF_CHUNK, (j + 1) * FF_CHUNK)
        gt = _dot(xb, wg_ref[:, sl])
        up = _dot(xb, wu_ref[:, sl])
        a_ref[:, sl] = (gt * jax.nn.sigmoid(gt) * up).astype(_BF)
    y = x + 0.5 * gate * _dot(a_ref[...], wd_ref[...])
    if final:
        y = y * lax.rsqrt(jnp.mean(y * y, axis=-1, keepdims=True) + EPS) * gf_ref[...]
    o_ref[...] = y


def _ffn(h, mod, g, wg, wu, wd, *, which, tm, tiles_per_mod, mod_base, g_final=None):
    t, d = h.shape
    d_ff = wg.shape[1]
    final = g_final is not None
    if tiles_per_mod is None:
        mod_map = lambda i: (mod_base, 0, 0)
    else:
        mod_map = lambda i: (i // tiles_per_mod, 0, 0)
    in_specs = [pl.BlockSpec((tm, d), lambda i: (i, 0)),
                pl.BlockSpec((1, N_MOD, d), mod_map),
                _const_spec((1, d)),
                _const_spec((d, d_ff)), _const_spec((d, d_ff)), _const_spec((d_ff, d))]
    args = [h, mod, g, wg, wu, wd]
    if final:
        in_specs.append(_const_spec((1, d)))
        args.append(g_final)
    return pl.pallas_call(
        functools.partial(_ffn_kernel, which=which, final=final),
        out_shape=jax.ShapeDtypeStruct((t, d), _F32),
        grid=(t // tm,),
        in_specs=in_specs,
        out_specs=pl.BlockSpec((tm, d), lambda i: (i, 0)),
        scratch_shapes=[pltpu.VMEM((tm, d_ff), _BF)],
        compiler_params=_params(1),
        name="ffn",
    )(*args)


def _in_proj_kernel(h_ref, mod_ref, g_ref, win_ref, gq_ref, wuq_ref, gkv_ref, wukv_ref, *rest, rope):
    if rope:
        ca_ref, sa_ref, cw_ref, sw_ref = rest[:4]
        rest = rest[4:]
    qa_ref, ka_ref, vat_ref, bg_ref, cx_ref, qw_ref, kw_ref, vwt_ref, ud_ref = rest

    x = h_ref[...]
    xb = _norm_mod(x, g_ref[...], mod_ref[0, 3:4, :], mod_ref[0, 4:5, :]).astype(_BF)

    def seg(lo, width):
        return _dot(xb, win_ref[:, lo:lo + width])

    cq = seg(SEG_CQ, 256)
    cqn = cq * lax.rsqrt(jnp.sum(cq * cq, axis=-1, keepdims=True) * (1.0 / MLA_Q_RANK) + EPS) * gq_ref[...]
    qq = _dot(cqn.astype(_BF), wuq_ref[...])
    q = qq[:, :MLA_HEADS * LANES]
    if rope:
        ca = ca_ref[...]
        sa = sa_ref[...]
        q = (q * jnp.concatenate([ca] * MLA_HEADS, axis=1)
             + qq[:, MLA_HEADS * LANES:] * jnp.concatenate([sa] * MLA_HEADS, axis=1))
    qa_ref[...] = q.astype(_BF)

    ckv = seg(SEG_CKV, 128)
    ckvn = ckv * lax.rsqrt(jnp.mean(ckv * ckv, axis=-1, keepdims=True) + EPS) * gkv_ref[...]
    kv = _dot(ckvn.astype(_BF), wukv_ref[...])
    kr2 = seg(SEG_KR, 256)
    kr = kr2[:, :LANES]
    if rope:
        kr = kr * ca + kr2[:, LANES:] * sa
    ka_ref[...] = (kv[:, :MLA_HEADS * LANES] + jnp.concatenate([kr] * MLA_HEADS, axis=1)).astype(_BF)
    vat_ref[...] = kv[:, MLA_HEADS * LANES:].T.astype(_BF)

    sc = seg(SEG_SC, 768)
    bg_ref[...] = sc[:, :256]
    cx_ref[...] = sc[:, 256:512] * sc[:, 512:]

    wq2 = seg(SEG_WQ, 512)
    qw = wq2[:, :256]
    wk2 = seg(SEG_WK, 256)
    kw = wk2[:, :LANES]
    if rope:
        cw = cw_ref[...]
        sw = sw_ref[...]
        qw = qw * jnp.concatenate([cw, cw], axis=1) + wq2[:, 256:] * jnp.concatenate([sw, sw], axis=1)
        kw = kw * cw + wk2[:, LANES:] * sw
    qw_ref[...] = qw.astype(_BF)
    lane = lax.broadcasted_iota(jnp.int32, kw.shape, 1)
    low = lane < WA_HEAD_DIM
    kw_r = pltpu.roll(kw, WA_HEAD_DIM, axis=1)
    zero = jnp.zeros_like(kw)
    kw_ref[...] = jnp.concatenate(
        [jnp.where(low, kw, zero), jnp.where(low, zero, kw_r),
         jnp.where(low, kw_r, zero), jnp.where(low, zero, kw)], axis=1).astype(_BF)
    vwt_ref[...] = seg(SEG_WV, 128).T.astype(_BF)

    cf = seg(SEG_CF, 512)
    ud_ref[...] = cf[:, :256] * jax.nn.sigmoid(cf[:, 256:])


def _in_proj(h, mod, g, w, tabs, *, tm, tiles_per_mod, mod_base, seq_tiles):
    t, d = h.shape
    rope = tabs is not None
    if tiles_per_mod is None:
        mod_map = lambda i: (mod_base, 0, 0)
    else:
        mod_map = lambda i: (i // tiles_per_mod, 0, 0)
    in_specs = [pl.BlockSpec((tm, d), lambda i: (i, 0)),
                pl.BlockSpec((1, N_MOD, d), mod_map),
                _const_spec((1, d)),
                _const_spec((d, N_PACK)),
                _const_spec((1, 256)), _const_spec((256, 1024)),
                _const_spec((1, 128)), _const_spec((128, 768))]
    args = [h, mod, g, w["w_in"], w["g_q"], w["w_uq"], w["g_kv"], w["w_ukv"]]
    if rope:
        in_specs += [pl.BlockSpec((tm, LANES), lambda i: (i % seq_tiles, 0))] * 4
        args += list(tabs)
    row = lambda width: pl.BlockSpec((tm, width), lambda i: (i, 0))
    col = lambda height: pl.BlockSpec((height, tm), lambda i: (0, i))
    out_shape = (
        jax.ShapeDtypeStruct((t, 512), _BF),
        jax.ShapeDtypeStruct((t, 512), _BF),
        jax.ShapeDtypeStruct((256, t), _BF),
        jax.ShapeDtypeStruct((t, 256), _F32),
        jax.ShapeDtypeStruct((t, 256), _F32),
        jax.ShapeDtypeStruct((t, 256), _BF),
        jax.ShapeDtypeStruct((t, 512), _BF),
        jax.ShapeDtypeStruct((128, t), _BF),
        jax.ShapeDtypeStruct((t, 256), _F32),
    )
    out_specs = (row(512), row(512), col(256), row(256), row(256), row(256), row(512), col(128), row(256))
    return pl.pallas_call(
        functools.partial(_in_proj_kernel, rope=rope),
        out_shape=out_shape,
        grid=(t // tm,),
        in_specs=in_specs,
        out_specs=out_specs,
        compiler_params=_params(1),
        name="in_proj",
    )(*args)


def _mla_kernel(q_ref, *refs, has_lat, n_lat_tiles):
    if has_lat:
        kl_ref, vl_ref, kc_ref, vc_ref, o_ref, s_ref, p_ref = refs
    else:
        kc_ref, vc_ref, o_ref, s_ref, p_ref = refs
    tq = q_ref.shape[0]
    n_ctx = kc_ref.shape[0]

    def softmax_stage(tk, tile_max, carry):
        stats = []
        for h in range(MLA_HEADS):
            m = carry[h][0]
            m_new = jnp.maximum(m, tile_max[h])
            stats.append((m_new, jnp.exp2(m - m_new)))
            for r0 in range(0, tk, SM_ROWS):
                p = jnp.exp2(s_ref[h, r0:r0 + SM_ROWS, :] - m_new)
                p_ref[h, r0:r0 + SM_ROWS, :] = p.astype(_BF)
        return stats

    def scores_stage(k_ref, key_slice, tk):
        tile_max = []
        for h in range(MLA_HEADS):
            s = _dot_nt(k_ref[key_slice, h * LANES:(h + 1) * LANES],
                        q_ref[:, h * LANES:(h + 1) * LANES])
            s_ref[h, 0:tk, :] = s
            tile_max.append(jnp.max(s, axis=0, keepdims=True))
        return tuple(tile_max)

    def value_stage(vt_ref, key_slice, tk, stats, carry):
        ones = jnp.ones((ONES_ROWS, tk), _BF)
        new = []
        for h in range(MLA_HEADS):
            m_new, alpha = stats[h]
            v1 = jnp.concatenate([vt_ref[h * MLA_V:(h + 1) * MLA_V, key_slice], ones], axis=0)
            new.append((m_new, alpha * carry[h][1] + _dot(v1, p_ref[h, 0:tk, :])))
        return tuple(new)

    carry = tuple((jnp.full((1, tq), NEG, _F32), jnp.zeros((MLA_V + ONES_ROWS, tq), _F32))
                  for _ in range(MLA_HEADS))
    if has_lat:
        def lat(t):
            return pl.ds(pl.multiple_of(t * TK_MLA, TK_MLA), TK_MLA)

        def body(t, state, last):
            tile_max, carry = state
            stats = softmax_stage(TK_MLA, tile_max, carry)
            if last:
                tile_max = scores_stage(kc_ref, slice(None), n_ctx)
            else:
                tile_max = scores_stage(kl_ref, lat(t + 1), TK_MLA)
            return tile_max, value_stage(vl_ref, lat(t), TK_MLA, stats, carry)

        state = (scores_stage(kl_ref, lat(0), TK_MLA), carry)
        state = lax.fori_loop(0, n_lat_tiles - 1, functools.partial(body, last=False), state)
        tile_max, carry = body(n_lat_tiles - 1, state, True)
    else:
        tile_max = scores_stage(kc_ref, slice(None), n_ctx)
    stats = softmax_stage(n_ctx, tile_max, carry)
    carry = value_stage(vc_ref, slice(None), n_ctx, stats, carry)
    outs = [acc[:MLA_V] / acc[MLA_V:MLA_V + 1] for (_, acc) in carry]
    o_ref[...] = jnp.concatenate(outs, axis=0).T.astype(_BF)


def _mla_attn(q, k_lat, vt_lat, k_ctx, vt_ctx, *, batch, tq):
    t = q.shape[0]
    per_b = t // batch
    nq = per_b // tq
    c = k_ctx.shape[0] // batch
    has_lat = k_lat is not None
    in_specs = [pl.BlockSpec((tq, 512), lambda b, j: (b * nq + j, 0))]
    args = [q]
    n_lat_tiles = 0
    if has_lat:
        s = k_lat.shape[0] // batch
        n_lat_tiles = s // TK_MLA
        in_specs += [pl.BlockSpec((s, 512), lambda b, j: (b, 0)),
                     pl.BlockSpec((256, s), lambda b, j: (0, b))]
        args += [k_lat, vt_lat]
    in_specs += [pl.BlockSpec((c, 512), lambda b, j: (b, 0)),
                 pl.BlockSpec((256, c), lambda b, j: (0, b))]
    args += [k_ctx, vt_ctx]
    return pl.pallas_call(
        functools.partial(_mla_kernel, has_lat=has_lat, n_lat_tiles=n_lat_tiles),
        out_shape=jax.ShapeDtypeStruct((t, 256), _BF),
        grid=(batch, nq),
        in_specs=in_specs,
        out_specs=pl.BlockSpec((tq, 256), lambda b, j: (b * nq + j, 0)),
        scratch_shapes=[pltpu.VMEM((MLA_HEADS, max(TK_MLA, c), tq), _F32),
                        pltpu.VMEM((MLA_HEADS, max(TK_MLA, c), tq), _BF)],
        compiler_params=_params(2),
        name="mla_attn",
    )(*args)


def _wa_kernel(sink_ref, q_ref, *refs, has_lat, seq_len):
    if has_lat:
        kl_ref, vl_ref, kc_ref, vc_ref, o_ref, sc_ref, pc_ref, sl_ref, pl_ref, bias_ref = refs
    else:
        kc_ref, vc_ref, o_ref, sc_ref, pc_ref = refs
    tq = q_ref.shape[0]
    n_ctx = kc_ref.shape[0]
    win_keys = tq + 2 * WINDOW
    if has_lat:
        q0 = pl.program_id(1) * tq
        start = pl.multiple_of(jnp.clip(q0 - WINDOW, 0, seq_len - win_keys), LANES)
        local = pl.ds(start, win_keys)
        kpos = start + lax.broadcasted_iota(jnp.int32, (win_keys, tq), 0)
        qpos = q0 + lax.broadcasted_iota(jnp.int32, (win_keys, tq), 1)
        bias_ref[...] = jnp.where(jnp.abs(kpos - qpos) <= WINDOW, 0.0, NEG)

    stats = []
    for hq in range(WA_HEADS):
        g = hq // (WA_HEADS // WA_KV_HEADS)
        lanes = slice(hq * LANES, (hq + 1) * LANES)
        qpair = q_ref[:, g * LANES:(g + 1) * LANES]
        snk = sink_ref[hq] * LOG2E
        s_ctx = _dot_nt(kc_ref[:, lanes], qpair)
        sc_ref[hq] = s_ctx
        m = jnp.maximum(jnp.max(s_ctx, axis=0, keepdims=True), snk)
        if has_lat:
            s_loc = _dot_nt(kl_ref[local, lanes], qpair) + bias_ref[...]
            sl_ref[hq] = s_loc
            m = jnp.maximum(m, jnp.max(s_loc, axis=0, keepdims=True))
        stats.append((m, jnp.exp2(snk - m)))

    for hq in range(WA_HEADS):
        m = stats[hq][0]
        for r0 in range(0, n_ctx, SM_ROWS):
            pc_ref[hq, r0:r0 + SM_ROWS, :] = jnp.exp2(sc_ref[hq, r0:r0 + SM_ROWS, :] - m).astype(_BF)
        if has_lat:
            for r0 in range(0, win_keys, SM_ROWS):
                pl_ref[hq, r0:r0 + SM_ROWS, :] = jnp.exp2(sl_ref[hq, r0:r0 + SM_ROWS, :] - m).astype(_BF)

    outs = []
    for hq in range(WA_HEADS):
        g = hq // (WA_HEADS // WA_KV_HEADS)
        rows = slice(g * WA_HEAD_DIM, (g + 1) * WA_HEAD_DIM)
        acc = _dot(jnp.concatenate([vc_ref[rows, :], jnp.ones((ONES_ROWS, n_ctx), _BF)], axis=0), pc_ref[hq])
        if has_lat:
            acc = acc + _dot(jnp.concatenate([vl_ref[rows, local], jnp.ones((ONES_ROWS, win_keys), _BF)], axis=0),
                             pl_ref[hq])
        l = acc[WA_HEAD_DIM:WA_HEAD_DIM + 1] + stats[hq][1]
        outs.append(acc[:WA_HEAD_DIM] / l)
    o_ref[...] = jnp.concatenate(outs, axis=0).T.astype(_BF)


def _wa_attn(sink, q, k_lat, vt_lat, k_ctx, vt_ctx, *, batch, tq):
    t = q.shape[0]
    per_b = t // batch
    nq = per_b // tq
    c = k_ctx.shape[0] // batch
    has_lat = k_lat is not None
    in_specs = [pl.BlockSpec(memory_space=pltpu.SMEM),
                pl.BlockSpec((tq, 256), lambda b, j: (b * nq + j, 0))]
    args = [sink, q]
    seq_len = 0
    if has_lat:
        seq_len = k_lat.shape[0] // batch
        in_specs += [pl.BlockSpec((seq_len, 512), lambda b, j: (b, 0)),
                     pl.BlockSpec((128, seq_len), lambda b, j: (0, b))]
        args += [k_lat, vt_lat]
    in_specs += [pl.BlockSpec((c, 512), lambda b, j: (b, 0)),
                 pl.BlockSpec((128, c), lambda b, j: (0, b))]
    args += [k_ctx, vt_ctx]
    return pl.pallas_call(
        functools.partial(_wa_kernel, has_lat=has_lat, seq_len=seq_len),
        out_shape=jax.ShapeDtypeStruct((t, 256), _BF),
        grid=(batch, nq),
        in_specs=in_specs,
        out_specs=pl.BlockSpec((tq, 256), lambda b, j: (b * nq + j, 0)),
        scratch_shapes=[pltpu.VMEM((WA_HEADS, c, tq), _F32), pltpu.VMEM((WA_HEADS, c, tq), _BF)]
                       + ([pltpu.VMEM((WA_HEADS, tq + 2 * WINDOW, tq), _F32),
                           pltpu.VMEM((WA_HEADS, tq + 2 * WINDOW, tq), _BF),
                           pltpu.VMEM((tq + 2 * WINDOW, tq), _F32)] if has_lat else []),
        compiler_params=_params(2),
        name="wa_attn",
    )(*args)


HALO_SC = 8
HALO_CF = 16
CONV_ROWS = 64
SC_PHASES = ((HALO_SC - SC_K // 2) % SUBLANES, (HALO_SC + SC_K // 2) % SUBLANES)
CF_PHASES = tuple(range(1, SUBLANES))


def _mix_out_kernel(h_ref, mod_ref, oa_ref, oc_ref, bg_ref, cx_ref, ud_ref, *rest, halo, seq_tiles):
    if halo:
        cxp_ref, cxn_ref, udp_ref, udn_ref = rest[:4]
        rest = rest[4:]
    (wsc_ref, wcf_ref, bcf_ref, gln_ref, bln_ref, wout_ref, o_ref,
     xsc_ref, xcf_ref, ssc_ref, scf_ref, ob_ref, od_ref) = rest
    tm = h_ref.shape[0]
    width = cx_ref.shape[1]

    if halo:
        i = pl.program_id(0) % seq_tiles
        has_prev = (i != 0).astype(_F32)
        has_next = (i != seq_tiles - 1).astype(_F32)
        xsc_ref[0:HALO_SC, :] = cxp_ref[...] * has_prev
        xsc_ref[HALO_SC + tm:, :] = cxn_ref[...] * has_next
        xcf_ref[0:HALO_CF, :] = udp_ref[...] * has_prev
        xcf_ref[HALO_CF + tm:, :] = udn_ref[...] * has_next
    else:
        xsc_ref[0:HALO_SC, :] = jnp.zeros((HALO_SC, width), _F32)
        xsc_ref[HALO_SC + tm:, :] = jnp.zeros((HALO_SC, width), _F32)
        xcf_ref[0:HALO_CF, :] = jnp.zeros((HALO_CF, width), _F32)
        xcf_ref[HALO_CF + tm:, :] = jnp.zeros((HALO_CF, width), _F32)
    xsc_ref[HALO_SC:HALO_SC + tm, :] = cx_ref[...]
    xcf_ref[HALO_CF:HALO_CF + tm, :] = ud_ref[...]

    for r in range(1, SUBLANES):
        scf_ref[r - 1] = xcf_ref[r:r + scf_ref.shape[1], :]
    for n, r in enumerate(SC_PHASES):
        ssc_ref[n] = xsc_ref[r:r + ssc_ref.shape[1], :]

    def tap(x_ref, copies_ref, phases, row):
        r = row % SUBLANES
        base = row - r
        if r == 0:
            return x_ref[base:base + CONV_ROWS, :]
        return copies_ref[phases.index(r), base:base + CONV_ROWS, :]

    for r0 in range(0, tm, CONV_ROWS):
        acc = jnp.zeros((CONV_ROWS, width), _F32)
        for k in range(SC_K):
            row = HALO_SC + r0 + k - SC_K // 2
            acc = acc + tap(xsc_ref, ssc_ref, SC_PHASES, row) * wsc_ref[k:k + 1, :]
        ob_ref[r0:r0 + CONV_ROWS, :] = (bg_ref[r0:r0 + CONV_ROWS, :] * acc).astype(_BF)

        acc = jnp.zeros((CONV_ROWS, width), _F32)
        for k in range(CF_K):
            row = HALO_CF + r0 + k - CF_K // 2
            acc = acc + tap(xcf_ref, scf_ref, CF_PHASES, row) * wcf_ref[k:k + 1, :]
        u = acc + bcf_ref[...]
        mu = jnp.mean(u, axis=-1, keepdims=True)
        uc = u - mu
        var = jnp.mean(uc * uc, axis=-1, keepdims=True)
        y = uc * lax.rsqrt(var + EPS) * gln_ref[...] + bln_ref[...]
        od_ref[r0:r0 + CONV_ROWS, :] = (y * jax.nn.sigmoid(y)).astype(_BF)

    mixed = jnp.concatenate([oa_ref[...], ob_ref[...], oc_ref[...], od_ref[...]], axis=1)
    o_ref[...] = h_ref[...] + mod_ref[0, 5:6, :] * _dot(mixed, wout_ref[...])


def _mix_out(h, mod, oa, oc, bg, cx, ud, w, *, tm, tiles_per_mod, mod_base, seq_tiles):
    t, d = h.shape
    halo = seq_tiles > 1
    if tiles_per_mod is None:
        mod_map = lambda i: (mod_base, 0, 0)
    else:
        mod_map = lambda i: (i // tiles_per_mod, 0, 0)
    row = lambda width: pl.BlockSpec((tm, width), lambda i: (i, 0))
    in_specs = [row(d), pl.BlockSpec((1, N_MOD, d), mod_map), row(256), row(256), row(256), row(256), row(256)]
    args = [h, mod, oa, oc, bg, cx, ud]
    if halo:
        nsc = tm // HALO_SC
        ncf = tm // HALO_CF
        last_sc = t // HALO_SC - 1
        last_cf = t // HALO_CF - 1
        in_specs += [
            pl.BlockSpec((HALO_SC, 256), lambda i: (jnp.maximum(i * nsc - 1, 0), 0)),
            pl.BlockSpec((HALO_SC, 256), lambda i: (jnp.minimum((i + 1) * nsc, last_sc), 0)),
            pl.BlockSpec((HALO_CF, 256), lambda i: (jnp.maximum(i * ncf - 1, 0), 0)),
            pl.BlockSpec((HALO_CF, 256), lambda i: (jnp.minimum((i + 1) * ncf, last_cf), 0)),
        ]
        args += [cx, cx, ud, ud]
    in_specs += [_const_spec((SC_K, 256)), _const_spec((CF_K, 256)), _const_spec((1, 256)),
                 _const_spec((1, 256)), _const_spec((1, 256)), _const_spec((d, d))]
    args += [w["w_sc"], w["w_cf"], w["b_cf"], w["g_ln"], w["b_ln"], w["w_out"]]
    return pl.pallas_call(
        functools.partial(_mix_out_kernel, halo=halo, seq_tiles=seq_tiles),
        out_shape=jax.ShapeDtypeStruct((t, d), _F32),
        grid=(t // tm,),
        in_specs=in_specs,
        out_specs=row(d),
        scratch_shapes=[pltpu.VMEM((tm + 2 * HALO_SC, 256), _F32),
                        pltpu.VMEM((tm + 2 * HALO_CF, 256), _F32),
                        pltpu.VMEM((len(SC_PHASES), tm + HALO_SC, 256), _F32),
                        pltpu.VMEM((len(CF_PHASES), tm + 2 * HALO_CF - SUBLANES, 256), _F32),
                        pltpu.VMEM((tm, 256), _BF),
                        pltpu.VMEM((tm, 256), _BF)],
        compiler_params=_params(1),
        name="mix_out",
    )(*args)


def _swap_halves(w, d_rot):
    m = d_rot // 4
    lead = w.shape[:-1]
    return jnp.flip(w.reshape(lead + (-1, 2, m)), axis=-2).reshape(w.shape)


def _rope_table(seq, d_rot, lanes_before):
    m = d_rot // 4
    d_ax = d_rot // 2
    rows = seq // GRID_W
    inv = ROPE_BASE ** (-jnp.arange(0, d_ax, 2, dtype=_F32) / d_ax)
    ar = jnp.arange(rows, dtype=_F32)[:, None] * inv[None, :]
    ac = jnp.arange(GRID_W, dtype=_F32)[:, None] * inv[None, :]

    def per_row(v):
        return jnp.broadcast_to(v[:, None, :], (rows, GRID_W, m)).reshape(seq, m)

    def per_col(v):
        return jnp.broadcast_to(v[None, :, :], (rows, GRID_W, m)).reshape(seq, m)

    cr, sr, cc, sc = per_row(jnp.cos(ar)), per_row(jnp.sin(ar)), per_col(jnp.cos(ac)), per_col(jnp.sin(ac))
    cos = jnp.concatenate([cr, cr, cc, cc], axis=1)
    sin = jnp.concatenate([-sr, sr, -sc, sc], axis=1)
    reps = (LANES - lanes_before) // d_rot if lanes_before == 0 else 1
    pad = LANES - lanes_before - reps * d_rot
    c_tab = jnp.concatenate([jnp.ones((seq, lanes_before), _F32)] + [cos] * reps + [jnp.ones((seq, pad), _F32)], axis=1)
    s_tab = jnp.concatenate([jnp.zeros((seq, lanes_before), _F32)] + [sin] * reps + [jnp.zeros((seq, pad), _F32)], axis=1)
    return c_tab, s_tab


def _pack_layer(p, l):
    offs = np.concatenate([[0], np.cumsum(IN_SIZES)])
    o_cq, o_ckv, o_kr, o_sc, o_wq, o_wkv, o_cf, o_end = (int(v) for v in offs)
    scale_a = float((MLA_NOPE + MLA_ROPE) ** -0.5 * LOG2E)
    scale_w = float(WA_HEAD_DIM ** -0.5 * LOG2E)
    w = p["w_in"][l]
    d = w.shape[0]
    zeros = lambda n: jnp.zeros((d, n), _F32)

    kr = w[:, o_kr:o_sc]
    wq = w[:, o_wq:o_wkv] * scale_w
    nk = WA_KV_HEADS * WA_HEAD_DIM
    wk = w[:, o_wkv:o_wkv + nk]
    pad_r = LANES - MLA_NOPE - MLA_ROPE
    w_in = jnp.concatenate([
        w[:, o_cq:o_ckv], zeros(256 - MLA_Q_RANK),
        zeros(MLA_NOPE), kr, zeros(pad_r),
        zeros(MLA_NOPE), _swap_halves(kr, MLA_ROPE), zeros(pad_r),
        w[:, o_sc:o_wq],
        wq, _swap_halves(wq, WA_HEAD_DIM),
        w[:, o_cf:o_end],
        wk, _swap_halves(wk, WA_HEAD_DIM),
        w[:, o_wkv + nk:o_cf],
        w[:, o_ckv:o_kr],
    ], axis=1).astype(_BF)

    wu = p["w_mla_uq"][l].reshape(MLA_Q_RANK, MLA_HEADS, MLA_NOPE + MLA_ROPE) * scale_a
    nope, rope = wu[..., :MLA_NOPE], wu[..., MLA_NOPE:]
    zq = lambda n: jnp.zeros((MLA_Q_RANK, MLA_HEADS, n), _F32)
    w_uq = jnp.concatenate([
        jnp.concatenate([nope, rope, zq(pad_r)], axis=-1).reshape(MLA_Q_RANK, MLA_HEADS * LANES),
        jnp.concatenate([zq(MLA_NOPE), _swap_halves(rope, MLA_ROPE), zq(pad_r)],
                        axis=-1).reshape(MLA_Q_RANK, MLA_HEADS * LANES),
    ], axis=1)
    w_uq = jnp.pad(w_uq, ((0, 256 - MLA_Q_RANK), (0, 0))).astype(_BF)
    g_q = jnp.pad(p["g_mla_q"][l], (0, 256 - MLA_Q_RANK)).reshape(1, 256)

    wkv = p["w_mla_ukv"][l].reshape(MLA_KV_RANK, MLA_HEADS, MLA_NOPE + MLA_V)
    k_nope = jnp.concatenate(
        [wkv[..., :MLA_NOPE], jnp.zeros((MLA_KV_RANK, MLA_HEADS, LANES - MLA_NOPE), _F32)], axis=-1)
    w_ukv = jnp.concatenate([k_nope.reshape(MLA_KV_RANK, MLA_HEADS * LANES),
                             wkv[..., MLA_NOPE:].reshape(MLA_KV_RANK, MLA_HEADS * MLA_V)], axis=1).astype(_BF)

    return dict(
        w_in=w_in, w_uq=w_uq, g_q=g_q, w_ukv=w_ukv, g_kv=p["g_mla_kv"][l].reshape(1, MLA_KV_RANK),
        w_sc=p["w_sc_conv"][l], w_cf=p["w_cf_conv"][l], b_cf=p["b_cf_conv"][l].reshape(1, -1),
        g_ln=p["g_cf_ln"][l].reshape(1, -1), b_ln=p["b_cf_ln"][l].reshape(1, -1),
        w_out=p["w_out"][l].astype(_BF),
        g_ffn1=p["g_ffn1"][l].reshape(1, -1), g_mix=p["g_mix"][l].reshape(1, -1),
        g_ffn2=p["g_ffn2"][l].reshape(1, -1),
        w1=(p["w1_gate"][l].astype(_BF), p["w1_up"][l].astype(_BF), p["w1_down"][l].astype(_BF)),
        w2=(p["w2_gate"][l].astype(_BF), p["w2_up"][l].astype(_BF), p["w2_down"][l].astype(_BF)),
        sink=p["wa_sink"][l],
    )


def kernel(x, c, ctx, c_ctx, w_mod, b_mod, g_ffn1, w1_gate, w1_up, w1_down, g_mix, w_in, g_mla_q, w_mla_uq, g_mla_kv, w_mla_ukv, w_sc_conv, wa_sink, w_cf_conv, b_cf_conv, g_cf_ln, b_cf_ln, w_out, g_ffn2, w2_gate, w2_up, w2_down, g_final):
    b, s, d = x.shape
    n_ctx = ctx.shape[1]
    depth = w_mod.shape[0]
    p = dict(w_in=w_in, g_mla_q=g_mla_q, w_mla_uq=w_mla_uq, g_mla_kv=g_mla_kv, w_mla_ukv=w_mla_ukv,
             w_sc_conv=w_sc_conv, wa_sink=wa_sink, w_cf_conv=w_cf_conv, b_cf_conv=b_cf_conv,
             g_cf_ln=g_cf_ln, b_cf_ln=b_cf_ln, w_out=w_out, g_ffn1=g_ffn1, g_mix=g_mix, g_ffn2=g_ffn2,
             w1_gate=w1_gate, w1_up=w1_up, w1_down=w1_down, w2_gate=w2_gate, w2_up=w2_up, w2_down=w2_down)

    tm = TM_LAT
    seq_tiles = s // tm
    tabs = _rope_table(s, MLA_ROPE, MLA_NOPE) + _rope_table(s, WA_HEAD_DIM, 0)

    mod_rows = 8
    cs = jnp.concatenate([c, c_ctx[None, :], jnp.zeros((mod_rows - b - 1, d), _F32)], axis=0)
    mod_all = _modulation(cs, w_mod, b_mod).reshape(depth, mod_rows, N_MOD, d)

    lat = dict(tm=tm, tiles_per_mod=seq_tiles, mod_base=0)
    cxt = dict(tm=n_ctx, tiles_per_mod=None, mod_base=b)

    h_lat = x.reshape(b * s, d)
    h_ctx = ctx.reshape(b * n_ctx, d)
    for l in range(depth):
        last = l == depth - 1
        w = _pack_layer(p, l)
        mod = mod_all[l]

        h_lat = _ffn(h_lat, mod, w["g_ffn1"], *w["w1"], which=0, **lat)
        h_ctx = _ffn(h_ctx, mod, w["g_ffn1"], *w["w1"], which=0, **cxt)

        qa, ka, vat, bg, cx, qw, kw, vwt, ud = _in_proj(h_lat, mod, w["g_mix"], w, tabs, seq_tiles=seq_tiles, **lat)
        qa_c, ka_c, vat_c, bg_c, cx_c, qw_c, kw_c, vwt_c, ud_c = _in_proj(
            h_ctx, mod, w["g_mix"], w, None, seq_tiles=1, **cxt)

        oa = _mla_attn(qa, ka, vat, ka_c, vat_c, batch=b, tq=TQ_MLA)
        oc = _wa_attn(w["sink"], qw, kw, vwt, kw_c, vwt_c, batch=b, tq=TQ_WA)
        h_lat = _mix_out(h_lat, mod, oa, oc, bg, cx, ud, w, seq_tiles=seq_tiles, **lat)

        if not last:
            oa_c = _mla_attn(qa_c, None, None, ka_c, vat_c, batch=b, tq=n_ctx)
            oc_c = _wa_attn(w["sink"], qw_c, None, None, kw_c, vwt_c, batch=b, tq=n_ctx)
            h_ctx = _mix_out(h_ctx, mod, oa_c, oc_c, bg_c, cx_c, ud_c, w, seq_tiles=1, **cxt)
            h_ctx = _ffn(h_ctx, mod, w["g_ffn2"], *w["w2"], which=2, **cxt)

        h_lat = _ffn(h_lat, mod, w["g_ffn2"], *w["w2"], which=2,
                     g_final=g_final.reshape(1, d) if last else None, **lat)

    return h_lat.reshape(b, s, d)
```

```python
import functools

import numpy as np
import jax
import jax.numpy as jnp
from jax import lax
from jax.experimental import pallas as pl
from jax.experimental.pallas import tpu as pltpu

GRID_W = 64
ROPE_BASE = 10000.0
EPS = 1e-6
NEG = -1e30
N_MOD = 9
LOG2E = 1.4426950408889634

MLA_HEADS = 4
MLA_NOPE = 64
MLA_ROPE = 32
MLA_V = 64
MLA_Q_RANK = 192
MLA_KV_RANK = 128
SC_WIDTH = 256
SC_K = 3
WA_HEADS = 4
WA_KV_HEADS = 2
WA_HEAD_DIM = 64
WINDOW = 128
CF_WIDTH = 256
CF_K = 31
IN_SIZES = (MLA_Q_RANK, MLA_KV_RANK, MLA_ROPE, 3 * SC_WIDTH,
            WA_HEADS * WA_HEAD_DIM, 2 * WA_KV_HEADS * WA_HEAD_DIM, 2 * CF_WIDTH)

LANES = 128
SUBLANES = 8
MXU_N = 256
SM_ROWS = 32
ONES_ROWS = 16

_BF = jnp.bfloat16
_F32 = jnp.float32

SEG_CQ = 0
SEG_KR = 256
SEG_SC = 512
SEG_WQ = 1280
SEG_CF = 1792
SEG_WK = 2304
SEG_WV = 2560
SEG_CKV = 2688
N_PACK = 2816

TM_LAT = 512
TQ_MLA = 512
TK_MLA = 1024
TQ_WA = 256
WA_SUBTILES = 2
FF_CHUNK = 256
CAST_ROWS = 256
VMEM_LIMIT = 52 * 1024 * 1024


def _params(n_axes, flags=None):
    return pltpu.CompilerParams(dimension_semantics=("arbitrary",) * n_axes,
                                vmem_limit_bytes=VMEM_LIMIT, flags=flags)


def _const_spec(shape):
    nd = len(shape)
    return pl.BlockSpec(shape, lambda *_: (0,) * nd, pipeline_mode=pl.Buffered(1))


def _layer_spec(shape, layer):
    return pl.BlockSpec((1,) + tuple(shape), lambda *_: (layer, 0, 0), pipeline_mode=pl.Buffered(1))


def _cast_kernel(x_ref, o_ref):
    o_ref[...] = x_ref[...].astype(o_ref.dtype)


def _cast_bf16(w):
    depth, rows, cols = w.shape
    tr = CAST_ROWS
    return pl.pallas_call(
        _cast_kernel,
        out_shape=jax.ShapeDtypeStruct(w.shape, _BF),
        grid=(depth, rows // tr),
        in_specs=[pl.BlockSpec((1, tr, cols), lambda l, i: (l, i, 0))],
        out_specs=pl.BlockSpec((1, tr, cols), lambda l, i: (l, i, 0)),
        compiler_params=_params(2),
        name="cast_bf16",
    )(w)


def _dot(a, b):
    return jnp.dot(a, b, preferred_element_type=_F32)


def _dot_nt(a, b):
    return lax.dot_general(a, b, (((1,), (1,)), ((), ())), preferred_element_type=_F32)


def _norm_mod(x, g, shift, scale):
    y = x * lax.rsqrt(jnp.mean(x * x, axis=-1, keepdims=True) + EPS) * g
    return y * (1.0 + scale) + shift


def _mod_kernel(c_ref, w_ref, b_ref, o_ref):
    c = c_ref[...]
    a = c * jax.nn.sigmoid(c)
    o_ref[0] = jnp.dot(a, w_ref[0], preferred_element_type=_F32,
                       precision=lax.Precision.HIGHEST) + b_ref[0]


def _modulation(cs, w_mod, b_mod):
    depth, d, n = w_mod.shape
    tn = 1024
    rows = cs.shape[0]
    return pl.pallas_call(
        _mod_kernel,
        out_shape=jax.ShapeDtypeStruct((depth, rows, n), _F32),
        grid=(depth, n // tn),
        in_specs=[pl.BlockSpec((rows, d), lambda l, j: (0, 0)),
                  pl.BlockSpec((1, d, tn), lambda l, j: (l, 0, j)),
                  pl.BlockSpec((1, 1, tn), lambda l, j: (l, 0, j))],
        out_specs=pl.BlockSpec((1, rows, tn), lambda l, j: (l, 0, j)),
        compiler_params=_params(2),
        name="modulation",
    )(cs, w_mod, b_mod.reshape(depth, 1, n))


def _ffn_kernel(h_ref, mod_ref, g_ref, wg_ref, wu_ref, wd_ref, *rest, which, final):
    if final:
        gf_ref, o_ref, a_ref = rest
    else:
        o_ref, a_ref = rest
    x = h_ref[...]
    shift = mod_ref[0, 3 * which:3 * which + 1, :]
    scale = mod_ref[0, 3 * which + 1:3 * which + 2, :]
    gate = mod_ref[0, 3 * which + 2:3 * which + 3, :]
    xb = _norm_mod(x, g_ref[...], shift, scale).astype(_BF)
    d_ff = wg_ref.shape[2]
    for j in range(d_ff // FF_CHUNK):
        sl = slice(j * FF_CHUNK, (j + 1) * FF_CHUNK)
        gt = _dot(xb, wg_ref[0, :, sl])
        up = _dot(xb, wu_ref[0, :, sl])
        a_ref[:, sl] = (gt * jax.nn.sigmoid(gt) * up).astype(_BF)
    y = x + 0.5 * gate * _dot(a_ref[...], wd_ref[0])
    if final:
        y = y * lax.rsqrt(jnp.mean(y * y, axis=-1, keepdims=True) + EPS) * gf_ref[...]
    o_ref[...] = y


def _ffn(h, mod, g, wg, wu, wd, *, layer, which, tm, tiles_per_mod, mod_base, g_final=None):
    t, d = h.shape
    d_ff = wg.shape[2]
    final = g_final is not None
    if tiles_per_mod is None:
        mod_map = lambda i: (mod_base, 0, 0)
    else:
        mod_map = lambda i: (i // tiles_per_mod, 0, 0)
    in_specs = [pl.BlockSpec((tm, d), lambda i: (i, 0)),
                pl.BlockSpec((1, N_MOD, d), mod_map),
                _const_spec((1, d)),
                _layer_spec((d, d_ff), layer), _layer_spec((d, d_ff), layer), _layer_spec((d_ff, d), layer)]
    args = [h, mod, g, wg, wu, wd]
    if final:
        in_specs.append(_const_spec((1, d)))
        args.append(g_final)
    return pl.pallas_call(
        functools.partial(_ffn_kernel, which=which, final=final),
        out_shape=jax.ShapeDtypeStruct((t, d), _F32),
        grid=(t // tm,),
        in_specs=in_specs,
        out_specs=pl.BlockSpec((tm, d), lambda i: (i, 0)),
        scratch_shapes=[pltpu.VMEM((tm, d_ff), _BF)],
        compiler_params=_params(1),
        name="ffn",
    )(*args)


def _in_proj_kernel(h_ref, mod_ref, g_ref, win_ref, gq_ref, wuq_ref, gkv_ref, wukv_ref, *rest, rope):
    if rope:
        ca_ref, sa_ref, cw_ref, sw_ref = rest[:4]
        rest = rest[4:]
    qa_ref, ka_ref, vat_ref, bg_ref, cx_ref, qw_ref, kw_ref, vwt_ref, ud_ref = rest

    x = h_ref[...]
    xb = _norm_mod(x, g_ref[...], mod_ref[0, 3:4, :], mod_ref[0, 4:5, :]).astype(_BF)

    def seg(lo, width):
        return _dot(xb, win_ref[:, lo:lo + width])

    cq = seg(SEG_CQ, 256)
    cqn = cq * lax.rsqrt(jnp.sum(cq * cq, axis=-1, keepdims=True) * (1.0 / MLA_Q_RANK) + EPS) * gq_ref[...]
    qq = _dot(cqn.astype(_BF), wuq_ref[...])
    q = qq[:, :MLA_HEADS * LANES]
    if rope:
        ca = ca_ref[...]
        sa = sa_ref[...]
        q = (q * jnp.concatenate([ca] * MLA_HEADS, axis=1)
             + qq[:, MLA_HEADS * LANES:] * jnp.concatenate([sa] * MLA_HEADS, axis=1))
    qa_ref[...] = q.astype(_BF)

    ckv = seg(SEG_CKV, 128)
    ckvn = ckv * lax.rsqrt(jnp.mean(ckv * ckv, axis=-1, keepdims=True) + EPS) * gkv_ref[...]
    kv = _dot(ckvn.astype(_BF), wukv_ref[...])
    kr2 = seg(SEG_KR, 256)
    kr = kr2[:, :LANES]
    if rope:
        kr = kr * ca + kr2[:, LANES:] * sa
    ka_ref[...] = (kv[:, :MLA_HEADS * LANES] + jnp.concatenate([kr] * MLA_HEADS, axis=1)).astype(_BF)
    vat_ref[...] = kv[:, MLA_HEADS * LANES:].T.astype(_BF)

    sc = seg(SEG_SC, 768)
    bg_ref[...] = sc[:, :256]
    cx_ref[...] = sc[:, 256:512] * sc[:, 512:]

    wq2 = seg(SEG_WQ, 512)
    qw = wq2[:, :256]
    wk2 = seg(SEG_WK, 256)
    kw = wk2[:, :LANES]
    if rope:
        cw = cw_ref[...]
        sw = sw_ref[...]
        qw = qw * jnp.concatenate([cw, cw], axis=1) + wq2[:, 256:] * jnp.concatenate([sw, sw], axis=1)
        kw = kw * cw + wk2[:, LANES:] * sw
    qw_ref[...] = qw.astype(_BF)
    lane = lax.broadcasted_iota(jnp.int32, kw.shape, 1)
    low = lane < WA_HEAD_DIM
    kw_r = pltpu.roll(kw, WA_HEAD_DIM, axis=1)
    zero = jnp.zeros_like(kw)
    kw_ref[...] = jnp.concatenate(
        [jnp.where(low, kw, zero), jnp.where(low, zero, kw_r),
         jnp.where(low, kw_r, zero), jnp.where(low, zero, kw)], axis=1).astype(_BF)
    vwt_ref[...] = seg(SEG_WV, 128).T.astype(_BF)

    cf = seg(SEG_CF, 512)
    ud_ref[...] = cf[:, :256] * jax.nn.sigmoid(cf[:, 256:])


def _in_proj(h, mod, g, w, tabs, *, tm, tiles_per_mod, mod_base, seq_tiles):
    t, d = h.shape
    rope = tabs is not None
    if tiles_per_mod is None:
        mod_map = lambda i: (mod_base, 0, 0)
    else:
        mod_map = lambda i: (i // tiles_per_mod, 0, 0)
    in_specs = [pl.BlockSpec((tm, d), lambda i: (i, 0)),
                pl.BlockSpec((1, N_MOD, d), mod_map),
                _const_spec((1, d)),
                _const_spec((d, N_PACK)),
                _const_spec((1, 256)), _const_spec((256, 1024)),
                _const_spec((1, 128)), _const_spec((128, 768))]
    args = [h, mod, g, w["w_in"], w["g_q"], w["w_uq"], w["g_kv"], w["w_ukv"]]
    if rope:
        in_specs += [pl.BlockSpec((tm, LANES), lambda i: (i % seq_tiles, 0))] * 4
        args += list(tabs)
    row = lambda width: pl.BlockSpec((tm, width), lambda i: (i, 0))
    col = lambda height: pl.BlockSpec((height, tm), lambda i: (0, i))
    out_shape = (
        jax.ShapeDtypeStruct((t, 512), _BF),
        jax.ShapeDtypeStruct((t, 512), _BF),
        jax.ShapeDtypeStruct((256, t), _BF),
        jax.ShapeDtypeStruct((t, 256), _F32),
        jax.ShapeDtypeStruct((t, 256), _F32),
        jax.ShapeDtypeStruct((t, 256), _BF),
        jax.ShapeDtypeStruct((t, 512), _BF),
        jax.ShapeDtypeStruct((128, t), _BF),
        jax.ShapeDtypeStruct((t, 256), _F32),
    )
    out_specs = (row(512), row(512), col(256), row(256), row(256), row(256), row(512), col(128), row(256))
    return pl.pallas_call(
        functools.partial(_in_proj_kernel, rope=rope),
        out_shape=out_shape,
        grid=(t // tm,),
        in_specs=in_specs,
        out_specs=out_specs,
        compiler_params=_params(1),
        name="in_proj",
    )(*args)


def _mla_kernel(q_ref, *refs, has_lat, n_lat_tiles):
    if has_lat:
        kl_ref, vl_ref, kc_ref, vc_ref, o_ref, s_ref, p_ref = refs
    else:
        kc_ref, vc_ref, o_ref, s_ref, p_ref = refs
    tq = q_ref.shape[0]
    n_ctx = kc_ref.shape[0]

    def scores_stage(k_ref, key_slice, tk):
        tile_max = []
        for h in range(MLA_HEADS):
            s = _dot_nt(k_ref[key_slice, h * LANES:(h + 1) * LANES],
                        q_ref[:, h * LANES:(h + 1) * LANES])
            s_ref[h, 0:tk, :] = s
            tile_max.append(jnp.max(s, axis=0, keepdims=True))
        return tuple(tile_max)

    def softmax_stage(tk, tile_max, ms):
        new_ms, alphas = [], []
        for h in range(MLA_HEADS):
            m_new = jnp.maximum(ms[h], tile_max[h])
            new_ms.append(m_new)
            alphas.append(jnp.exp2(ms[h] - m_new))
            for r0 in range(0, tk, SM_ROWS):
                p = jnp.exp2(s_ref[h, r0:r0 + SM_ROWS, :] - m_new)
                p_ref[h, r0:r0 + SM_ROWS, :] = p.astype(_BF)
        return tuple(new_ms), tuple(alphas)

    def value_stage(vt_ref, key_slice, tk, alphas, accs):
        ones = jnp.ones((ONES_ROWS, tk), _BF)
        new = []
        for h in range(MLA_HEADS):
            v1 = jnp.concatenate([vt_ref[h * MLA_V:(h + 1) * MLA_V, key_slice], ones], axis=0)
            new.append(alphas[h] * accs[h] + _dot(v1, p_ref[h, 0:tk, :]))
        return tuple(new)

    ms = tuple(jnp.full((1, tq), NEG, _F32) for _ in range(MLA_HEADS))
    accs = tuple(jnp.zeros((MLA_V + ONES_ROWS, tq), _F32) for _ in range(MLA_HEADS))
    if has_lat:
        def lat(t):
            return pl.ds(pl.multiple_of(t * TK_MLA, TK_MLA), TK_MLA)

        def body(k, state, ctx_next):
            tile_max, ms, accs = state
            ms, alphas = softmax_stage(TK_MLA, tile_max, ms)
            if ctx_next:
                tile_max = scores_stage(kc_ref, slice(None), n_ctx)
            else:
                tile_max = scores_stage(kl_ref, lat(k + 1), TK_MLA)
            return tile_max, ms, value_stage(vl_ref, lat(k), TK_MLA, alphas, accs)

        state = (scores_stage(kl_ref, lat(0), TK_MLA), ms, accs)
        state = lax.fori_loop(0, n_lat_tiles - 1, functools.partial(body, ctx_next=False), state)
        tile_max, ms, accs = body(n_lat_tiles - 1, state, True)
    else:
        tile_max = scores_stage(kc_ref, slice(None), n_ctx)
    ms, alphas = softmax_stage(n_ctx, tile_max, ms)
    accs = value_stage(vc_ref, slice(None), n_ctx, alphas, accs)
    outs = [acc[:MLA_V] / acc[MLA_V:MLA_V + 1] for acc in accs]
    o_ref[...] = jnp.concatenate(outs, axis=0).T.astype(_BF)


def _mla_attn(q, k_lat, vt_lat, k_ctx, vt_ctx, *, batch, tq):
    t = q.shape[0]
    per_b = t // batch
    nq = per_b // tq
    c = k_ctx.shape[0] // batch
    has_lat = k_lat is not None
    in_specs = [pl.BlockSpec((tq, 512), lambda b, j: (b * nq + j, 0))]
    args = [q]
    n_lat_tiles = 0
    if has_lat:
        s = k_lat.shape[0] // batch
        n_lat_tiles = s // TK_MLA
        in_specs += [pl.BlockSpec((s, 512), lambda b, j: (b, 0)),
                     pl.BlockSpec((256, s), lambda b, j: (0, b))]
        args += [k_lat, vt_lat]
    in_specs += [pl.BlockSpec((c, 512), lambda b, j: (b, 0)),
                 pl.BlockSpec((256, c), lambda b, j: (0, b))]
    args += [k_ctx, vt_ctx]
    return pl.pallas_call(
        functools.partial(_mla_kernel, has_lat=has_lat, n_lat_tiles=n_lat_tiles),
        out_shape=jax.ShapeDtypeStruct((t, 256), _BF),
        grid=(batch, nq),
        in_specs=in_specs,
        out_specs=pl.BlockSpec((tq, 256), lambda b, j: (b * nq + j, 0)),
        scratch_shapes=[pltpu.VMEM((MLA_HEADS, max(TK_MLA, c), tq), _F32),
                        pltpu.VMEM((MLA_HEADS, max(TK_MLA, c), tq), _BF)],
        compiler_params=_params(2),
        name="mla_attn",
    )(*args)


def _wa_kernel(sink_ref, q_ref, *refs, has_lat, seq_len, n_sub):
    if has_lat:
        kl_ref, vl_ref, kc_ref, vc_ref, o_ref, sc_ref, pc_ref, sl_ref, pl_ref, bias_ref = refs
    else:
        kc_ref, vc_ref, o_ref, sc_ref, pc_ref = refs
    tq = q_ref.shape[0] // n_sub
    n_ctx = kc_ref.shape[0]
    win_keys = tq + 2 * WINDOW
    units = [(sub, hq) for sub in range(n_sub) for hq in range(WA_HEADS)]
    local = []
    if has_lat:
        for sub in range(n_sub):
            q0 = (pl.program_id(1) * n_sub + sub) * tq
            start = pl.multiple_of(jnp.clip(q0 - WINDOW, 0, seq_len - win_keys), LANES)
            local.append(pl.ds(start, win_keys))
            kpos = start + lax.broadcasted_iota(jnp.int32, (win_keys, tq), 0)
            qpos = q0 + lax.broadcasted_iota(jnp.int32, (win_keys, tq), 1)
            bias_ref[sub] = jnp.where(jnp.abs(kpos - qpos) <= WINDOW, 0.0, NEG)

    stats = {}
    for sub, hq in units:
        g = hq // (WA_HEADS // WA_KV_HEADS)
        lanes = slice(hq * LANES, (hq + 1) * LANES)
        qpair = q_ref[sub * tq:(sub + 1) * tq, g * LANES:(g + 1) * LANES]
        snk = sink_ref[hq] * LOG2E
        s_ctx = _dot_nt(kc_ref[:, lanes], qpair)
        sc_ref[sub, hq] = s_ctx
        m = jnp.maximum(jnp.max(s_ctx, axis=0, keepdims=True), snk)
        if has_lat:
            s_loc = _dot_nt(kl_ref[local[sub], lanes], qpair) + bias_ref[sub]
            sl_ref[sub, hq] = s_loc
            m = jnp.maximum(m, jnp.max(s_loc, axis=0, keepdims=True))
        stats[sub, hq] = (m, jnp.exp2(snk - m))

    for sub, hq in units:
        m = stats[sub, hq][0]
        for r0 in range(0, n_ctx, SM_ROWS):
            pc_ref[sub, hq, r0:r0 + SM_ROWS, :] = jnp.exp2(sc_ref[sub, hq, r0:r0 + SM_ROWS, :] - m).astype(_BF)
        if has_lat:
            for r0 in range(0, win_keys, SM_ROWS):
                pl_ref[sub, hq, r0:r0 + SM_ROWS, :] = jnp.exp2(sl_ref[sub, hq, r0:r0 + SM_ROWS, :] - m).astype(_BF)

    for sub in range(n_sub):
        outs = []
        for hq in range(WA_HEADS):
            g = hq // (WA_HEADS // WA_KV_HEADS)
            rows = slice(g * WA_HEAD_DIM, (g + 1) * WA_HEAD_DIM)
            acc = _dot(jnp.concatenate([vc_ref[rows, :], jnp.ones((ONES_ROWS, n_ctx), _BF)], axis=0),
                       pc_ref[sub, hq])
            if has_lat:
                acc = acc + _dot(jnp.concatenate([vl_ref[rows, local[sub]],
                                                  jnp.ones((ONES_ROWS, win_keys), _BF)], axis=0), pl_ref[sub, hq])
            l = acc[WA_HEAD_DIM:WA_HEAD_DIM + 1] + stats[sub, hq][1]
            outs.append(acc[:WA_HEAD_DIM] / l)
        o_ref[sub * tq:(sub + 1) * tq, :] = jnp.concatenate(outs, axis=0).T.astype(_BF)


def _wa_attn(sink, q, k_lat, vt_lat, k_ctx, vt_ctx, *, batch, tq, n_sub):
    t = q.shape[0]
    per_b = t // batch
    step_q = tq * n_sub
    nq = per_b // step_q
    c = k_ctx.shape[0] // batch
    has_lat = k_lat is not None
    in_specs = [pl.BlockSpec(memory_space=pltpu.SMEM),
                pl.BlockSpec((step_q, 256), lambda b, j: (b * nq + j, 0))]
    args = [sink, q]
    seq_len = 0
    if has_lat:
        seq_len = k_lat.shape[0] // batch
        in_specs += [pl.BlockSpec((seq_len, 512), lambda b, j: (b, 0)),
                     pl.BlockSpec((128, seq_len), lambda b, j: (0, b))]
        args += [k_lat, vt_lat]
    in_specs += [pl.BlockSpec((c, 512), lambda b, j: (b, 0)),
                 pl.BlockSpec((128, c), lambda b, j: (0, b))]
    args += [k_ctx, vt_ctx]
    win_keys = tq + 2 * WINDOW
    scratch = [pltpu.VMEM((n_sub, WA_HEADS, c, tq), _F32), pltpu.VMEM((n_sub, WA_HEADS, c, tq), _BF)]
    if has_lat:
        scratch += [pltpu.VMEM((n_sub, WA_HEADS, win_keys, tq), _F32),
                    pltpu.VMEM((n_sub, WA_HEADS, win_keys, tq), _BF),
                    pltpu.VMEM((n_sub, win_keys, tq), _F32)]
    return pl.pallas_call(
        functools.partial(_wa_kernel, has_lat=has_lat, seq_len=seq_len, n_sub=n_sub),
        out_shape=jax.ShapeDtypeStruct((t, 256), _BF),
        grid=(batch, nq),
        in_specs=in_specs,
        out_specs=pl.BlockSpec((step_q, 256), lambda b, j: (b * nq + j, 0)),
        scratch_shapes=scratch,
        compiler_params=_params(2),
        name="wa_attn",
    )(*args)


HALO_SC = 8
HALO_CF = 16
CONV_ROWS = 64
SC_PHASES = ((HALO_SC - SC_K // 2) % SUBLANES, (HALO_SC + SC_K // 2) % SUBLANES)
CF_PHASES = tuple(range(1, SUBLANES))


def _mix_out_kernel(h_ref, mod_ref, oa_ref, oc_ref, bg_ref, cx_ref, ud_ref, *rest, halo, seq_tiles):
    if halo:
        cxp_ref, cxn_ref, udp_ref, udn_ref = rest[:4]
        rest = rest[4:]
    (wsc_ref, wcf_ref, bcf_ref, gln_ref, bln_ref, wout_ref, o_ref,
     xsc_ref, xcf_ref, ssc_ref, scf_ref, ob_ref, od_ref) = rest
    tm = h_ref.shape[0]
    width = cx_ref.shape[1]

    if halo:
        i = pl.program_id(0) % seq_tiles
        has_prev = (i != 0).astype(_F32)
        has_next = (i != seq_tiles - 1).astype(_F32)
        xsc_ref[0:HALO_SC, :] = cxp_ref[...] * has_prev
        xsc_ref[HALO_SC + tm:, :] = cxn_ref[...] * has_next
        xcf_ref[0:HALO_CF, :] = udp_ref[...] * has_prev
        xcf_ref[HALO_CF + tm:, :] = udn_ref[...] * has_next
    else:
        xsc_ref[0:HALO_SC, :] = jnp.zeros((HALO_SC, width), _F32)
        xsc_ref[HALO_SC + tm:, :] = jnp.zeros((HALO_SC, width), _F32)
        xcf_ref[0:HALO_CF, :] = jnp.zeros((HALO_CF, width), _F32)
        xcf_ref[HALO_CF + tm:, :] = jnp.zeros((HALO_CF, width), _F32)
    xsc_ref[HALO_SC:HALO_SC + tm, :] = cx_ref[...]
    xcf_ref[HALO_CF:HALO_CF + tm, :] = ud_ref[...]

    for r in range(1, SUBLANES):
        scf_ref[r - 1] = xcf_ref[r:r + scf_ref.shape[1], :]
    for n, r in enumerate(SC_PHASES):
        ssc_ref[n] = xsc_ref[r:r + ssc_ref.shape[1], :]

    def tap(x_ref, copies_ref, phases, row):
        r = row % SUBLANES
        base = row - r
        if r == 0:
            return x_ref[base:base + CONV_ROWS, :]
        return copies_ref[phases.index(r), base:base + CONV_ROWS, :]

    for r0 in range(0, tm, CONV_ROWS):
        acc = jnp.zeros((CONV_ROWS, width), _F32)
        for k in range(SC_K):
            row = HALO_SC + r0 + k - SC_K // 2
            acc = acc + tap(xsc_ref, ssc_ref, SC_PHASES, row) * wsc_ref[k:k + 1, :]
        ob_ref[r0:r0 + CONV_ROWS, :] = (bg_ref[r0:r0 + CONV_ROWS, :] * acc).astype(_BF)

        acc = jnp.zeros((CONV_ROWS, width), _F32)
        for k in range(CF_K):
            row = HALO_CF + r0 + k - CF_K // 2
            acc = acc + tap(xcf_ref, scf_ref, CF_PHASES, row) * wcf_ref[k:k + 1, :]
        u = acc + bcf_ref[...]
        mu = jnp.mean(u, axis=-1, keepdims=True)
        uc = u - mu
        var = jnp.mean(uc * uc, axis=-1, keepdims=True)
        y = uc * lax.rsqrt(var + EPS) * gln_ref[...] + bln_ref[...]
        od_ref[r0:r0 + CONV_ROWS, :] = (y * jax.nn.sigmoid(y)).astype(_BF)

    mixed = jnp.concatenate([oa_ref[...], ob_ref[...], oc_ref[...], od_ref[...]], axis=1)
    o_ref[...] = h_ref[...] + mod_ref[0, 5:6, :] * _dot(mixed, wout_ref[0])


def _mix_out(h, mod, oa, oc, bg, cx, ud, w, w_out, *, layer, tm, tiles_per_mod, mod_base, seq_tiles):
    t, d = h.shape
    halo = seq_tiles > 1
    if tiles_per_mod is None:
        mod_map = lambda i: (mod_base, 0, 0)
    else:
        mod_map = lambda i: (i // tiles_per_mod, 0, 0)
    row = lambda width: pl.BlockSpec((tm, width), lambda i: (i, 0))
    in_specs = [row(d), pl.BlockSpec((1, N_MOD, d), mod_map), row(256), row(256), row(256), row(256), row(256)]
    args = [h, mod, oa, oc, bg, cx, ud]
    if halo:
        nsc = tm // HALO_SC
        ncf = tm // HALO_CF
        last_sc = t // HALO_SC - 1
        last_cf = t // HALO_CF - 1
        in_specs += [
            pl.BlockSpec((HALO_SC, 256), lambda i: (jnp.maximum(i * nsc - 1, 0), 0)),
            pl.BlockSpec((HALO_SC, 256), lambda i: (jnp.minimum((i + 1) * nsc, last_sc), 0)),
            pl.BlockSpec((HALO_CF, 256), lambda i: (jnp.maximum(i * ncf - 1, 0), 0)),
            pl.BlockSpec((HALO_CF, 256), lambda i: (jnp.minimum((i + 1) * ncf, last_cf), 0)),
        ]
        args += [cx, cx, ud, ud]
    in_specs += [_const_spec((SC_K, 256)), _const_spec((CF_K, 256)), _const_spec((1, 256)),
                 _const_spec((1, 256)), _const_spec((1, 256)), _layer_spec((d, d), layer)]
    args += [w["w_sc"], w["w_cf"], w["b_cf"], w["g_ln"], w["b_ln"], w_out]
    return pl.pallas_call(
        functools.partial(_mix_out_kernel, halo=halo, seq_tiles=seq_tiles),
        out_shape=jax.ShapeDtypeStruct((t, d), _F32),
        grid=(t // tm,),
        in_specs=in_specs,
        out_specs=row(d),
        scratch_shapes=[pltpu.VMEM((tm + 2 * HALO_SC, 256), _F32),
                        pltpu.VMEM((tm + 2 * HALO_CF, 256), _F32),
                        pltpu.VMEM((len(SC_PHASES), tm + HALO_SC, 256), _F32),
                        pltpu.VMEM((len(CF_PHASES), tm + 2 * HALO_CF - SUBLANES, 256), _F32),
                        pltpu.VMEM((tm, 256), _BF),
                        pltpu.VMEM((tm, 256), _BF)],
        compiler_params=_params(1),
        name="mix_out",
    )(*args)


def _swap_halves(w, d_rot):
    m = d_rot // 4
    lead = w.shape[:-1]
    return jnp.flip(w.reshape(lead + (-1, 2, m)), axis=-2).reshape(w.shape)


def _rope_table(seq, d_rot, lanes_before):
    m = d_rot // 4
    d_ax = d_rot // 2
    rows = seq // GRID_W
    inv = ROPE_BASE ** (-jnp.arange(0, d_ax, 2, dtype=_F32) / d_ax)
    ar = jnp.arange(rows, dtype=_F32)[:, None] * inv[None, :]
    ac = jnp.arange(GRID_W, dtype=_F32)[:, None] * inv[None, :]

    def per_row(v):
        return jnp.broadcast_to(v[:, None, :], (rows, GRID_W, m)).reshape(seq, m)

    def per_col(v):
        return jnp.broadcast_to(v[None, :, :], (rows, GRID_W, m)).reshape(seq, m)

    cr, sr, cc, sc = per_row(jnp.cos(ar)), per_row(jnp.sin(ar)), per_col(jnp.cos(ac)), per_col(jnp.sin(ac))
    cos = jnp.concatenate([cr, cr, cc, cc], axis=1)
    sin = jnp.concatenate([-sr, sr, -sc, sc], axis=1)
    reps = (LANES - lanes_before) // d_rot if lanes_before == 0 else 1
    pad = LANES - lanes_before - reps * d_rot
    c_tab = jnp.concatenate([jnp.ones((seq, lanes_before), _F32)] + [cos] * reps + [jnp.ones((seq, pad), _F32)], axis=1)
    s_tab = jnp.concatenate([jnp.zeros((seq, lanes_before), _F32)] + [sin] * reps + [jnp.zeros((seq, pad), _F32)], axis=1)
    return c_tab, s_tab


def _pack_layer(p, l):
    offs = np.concatenate([[0], np.cumsum(IN_SIZES)])
    o_cq, o_ckv, o_kr, o_sc, o_wq, o_wkv, o_cf, o_end = (int(v) for v in offs)
    scale_a = float((MLA_NOPE + MLA_ROPE) ** -0.5 * LOG2E)
    scale_w = float(WA_HEAD_DIM ** -0.5 * LOG2E)
    w = p["w_in"][l]
    d = w.shape[0]
    zeros = lambda n: jnp.zeros((d, n), _F32)

    kr = w[:, o_kr:o_sc]
    wq = w[:, o_wq:o_wkv] * scale_w
    nk = WA_KV_HEADS * WA_HEAD_DIM
    wk = w[:, o_wkv:o_wkv + nk]
    pad_r = LANES - MLA_NOPE - MLA_ROPE
    w_in = jnp.concatenate([
        w[:, o_cq:o_ckv], zeros(256 - MLA_Q_RANK),
        zeros(MLA_NOPE), kr, zeros(pad_r),
        zeros(MLA_NOPE), _swap_halves(kr, MLA_ROPE), zeros(pad_r),
        w[:, o_sc:o_wq],
        wq, _swap_halves(wq, WA_HEAD_DIM),
        w[:, o_cf:o_end],
        wk, _swap_halves(wk, WA_HEAD_DIM),
        w[:, o_wkv + nk:o_cf],
        w[:, o_ckv:o_kr],
    ], axis=1).astype(_BF)

    wu = p["w_mla_uq"][l].reshape(MLA_Q_RANK, MLA_HEADS, MLA_NOPE + MLA_ROPE) * scale_a
    nope, rope = wu[..., :MLA_NOPE], wu[..., MLA_NOPE:]
    zq = lambda n: jnp.zeros((MLA_Q_RANK, MLA_HEADS, n), _F32)
    w_uq = jnp.concatenate([
        jnp.concatenate([nope, rope, zq(pad_r)], axis=-1).reshape(MLA_Q_RANK, MLA_HEADS * LANES),
        jnp.concatenate([zq(MLA_NOPE), _swap_halves(rope, MLA_ROPE), zq(pad_r)],
                        axis=-1).reshape(MLA_Q_RANK, MLA_HEADS * LANES),
    ], axis=1)
    w_uq = jnp.pad(w_uq, ((0, 256 - MLA_Q_RANK), (0, 0))).astype(_BF)
    g_q = jnp.pad(p["g_mla_q"][l], (0, 256 - MLA_Q_RANK)).reshape(1, 256)

    wkv = p["w_mla_ukv"][l].reshape(MLA_KV_RANK, MLA_HEADS, MLA_NOPE + MLA_V)
    k_nope = jnp.concatenate(
        [wkv[..., :MLA_NOPE], jnp.zeros((MLA_KV_RANK, MLA_HEADS, LANES - MLA_NOPE), _F32)], axis=-1)
    w_ukv = jnp.concatenate([k_nope.reshape(MLA_KV_RANK, MLA_HEADS * LANES),
                             wkv[..., MLA_NOPE:].reshape(MLA_KV_RANK, MLA_HEADS * MLA_V)], axis=1).astype(_BF)

    return dict(
        w_in=w_in, w_uq=w_uq, g_q=g_q, w_ukv=w_ukv, g_kv=p["g_mla_kv"][l].reshape(1, MLA_KV_RANK),
        w_sc=p["w_sc_conv"][l], w_cf=p["w_cf_conv"][l], b_cf=p["b_cf_conv"][l].reshape(1, -1),
        g_ln=p["g_cf_ln"][l].reshape(1, -1), b_ln=p["b_cf_ln"][l].reshape(1, -1),
        g_ffn1=p["g_ffn1"][l].reshape(1, -1), g_mix=p["g_mix"][l].reshape(1, -1),
        g_ffn2=p["g_ffn2"][l].reshape(1, -1),
        sink=p["wa_sink"][l],
    )


def kernel(x, c, ctx, c_ctx, w_mod, b_mod, g_ffn1, w1_gate, w1_up, w1_down, g_mix, w_in, g_mla_q, w_mla_uq, g_mla_kv, w_mla_ukv, w_sc_conv, wa_sink, w_cf_conv, b_cf_conv, g_cf_ln, b_cf_ln, w_out, g_ffn2, w2_gate, w2_up, w2_down, g_final):
    b, s, d = x.shape
    n_ctx = ctx.shape[1]
    depth = w_mod.shape[0]
    p = dict(w_in=w_in, g_mla_q=g_mla_q, w_mla_uq=w_mla_uq, g_mla_kv=g_mla_kv, w_mla_ukv=w_mla_ukv,
             w_sc_conv=w_sc_conv, wa_sink=wa_sink, w_cf_conv=w_cf_conv, b_cf_conv=b_cf_conv,
             g_cf_ln=g_cf_ln, b_cf_ln=b_cf_ln, w_out=w_out, g_ffn1=g_ffn1, g_mix=g_mix, g_ffn2=g_ffn2,
             w1_gate=w1_gate, w1_up=w1_up, w1_down=w1_down, w2_gate=w2_gate, w2_up=w2_up, w2_down=w2_down)

    tm = TM_LAT
    seq_tiles = s // tm
    tabs = _rope_table(s, MLA_ROPE, MLA_NOPE) + _rope_table(s, WA_HEAD_DIM, 0)

    mod_rows = 8
    cs = jnp.concatenate([c, c_ctx[None, :], jnp.zeros((mod_rows - b - 1, d), _F32)], axis=0)
    mod_all = _modulation(cs, w_mod, b_mod).reshape(depth, mod_rows, N_MOD, d)

    lat = dict(tm=tm, tiles_per_mod=seq_tiles, mod_base=0)
    cxt = dict(tm=n_ctx, tiles_per_mod=None, mod_base=b)

    w1 = (_cast_bf16(w1_gate), _cast_bf16(w1_up), _cast_bf16(w1_down))
    w2 = (_cast_bf16(w2_gate), _cast_bf16(w2_up), _cast_bf16(w2_down))
    w_out_b = _cast_bf16(w_out)

    h_lat = x.reshape(b * s, d)
    h_ctx = ctx.reshape(b * n_ctx, d)
    for l in range(depth):
        last = l == depth - 1
        w = _pack_layer(p, l)
        mod = mod_all[l]

        h_lat = _ffn(h_lat, mod, w["g_ffn1"], *w1, layer=l, which=0, **lat)
        h_ctx = _ffn(h_ctx, mod, w["g_ffn1"], *w1, layer=l, which=0, **cxt)

        qa, ka, vat, bg, cx, qw, kw, vwt, ud = _in_proj(h_lat, mod, w["g_mix"], w, tabs, seq_tiles=seq_tiles, **lat)
        qa_c, ka_c, vat_c, bg_c, cx_c, qw_c, kw_c, vwt_c, ud_c = _in_proj(
            h_ctx, mod, w["g_mix"], w, None, seq_tiles=1, **cxt)

        oa = _mla_attn(qa, ka, vat, ka_c, vat_c, batch=b, tq=TQ_MLA)
        oc = _wa_attn(w["sink"], qw, kw, vwt, kw_c, vwt_c, batch=b, tq=TQ_WA, n_sub=WA_SUBTILES)
        h_lat = _mix_out(h_lat, mod, oa, oc, bg, cx, ud, w, w_out_b, layer=l, seq_tiles=seq_tiles, **lat)

        if not last:
            oa_c = _mla_attn(qa_c, None, None, ka_c, vat_c, batch=b, tq=n_ctx)
            oc_c = _wa_attn(w["sink"], qw_c, None, None, kw_c, vwt_c, batch=b, tq=n_ctx, n_sub=1)
            h_ctx = _mix_out(h_ctx, mod, oa_c, oc_c, bg_c, cx_c, ud_c, w, w_out_b, layer=l, seq_tiles=1, **cxt)
            h_ctx = _ffn(h_ctx, mod, w["g_ffn2"], *w2, layer=l, which=2, **cxt)

        h_lat = _ffn(h_lat, mod, w["g_ffn2"], *w2, layer=l, which=2,
                     g_final=g_final.reshape(1, d) if last else None, **lat)

    return h_lat.reshape(b, s, d)
```

```python
import functools

import numpy as np
import jax
import jax.numpy as jnp
from jax import lax
from jax.experimental import pallas as pl
from jax.experimental.pallas import tpu as pltpu

GRID_W = 64
ROPE_BASE = 10000.0
EPS = 1e-6
NEG = -1e30
N_MOD = 9
LOG2E = 1.4426950408889634

MLA_HEADS = 4
MLA_NOPE = 64
MLA_ROPE = 32
MLA_V = 64
MLA_Q_RANK = 192
MLA_KV_RANK = 128
SC_WIDTH = 256
SC_K = 3
WA_HEADS = 4
WA_KV_HEADS = 2
WA_HEAD_DIM = 64
WINDOW = 128
CF_WIDTH = 256
CF_K = 31
IN_SIZES = (MLA_Q_RANK, MLA_KV_RANK, MLA_ROPE, 3 * SC_WIDTH,
            WA_HEADS * WA_HEAD_DIM, 2 * WA_KV_HEADS * WA_HEAD_DIM, 2 * CF_WIDTH)

LANES = 128
SUBLANES = 8
MXU_N = 256
SM_ROWS = 32
ONES_ROWS = 16

_BF = jnp.bfloat16
_F32 = jnp.float32

SEG_CQ = 0
SEG_KRC = 256
SEG_SC = 512
SEG_WQ = 1280
SEG_CF = 1536
SEG_WKV = 2048
N_PACK = 2304

TM_LAT = 512
TQ_MLA = 512
TK_MLA = 1024
TQ_WA = 256
WA_SUBTILES = 2
FF_CHUNK = 256
CAST_ROWS = 256
FFN_SUBTILES = 2
VMEM_LIMIT = 52 * 1024 * 1024


def _params(n_axes, flags=None):
    return pltpu.CompilerParams(dimension_semantics=("arbitrary",) * n_axes,
                                vmem_limit_bytes=VMEM_LIMIT, flags=flags)


def _const_spec(shape):
    nd = len(shape)
    return pl.BlockSpec(shape, lambda *_: (0,) * nd, pipeline_mode=pl.Buffered(1))


def _layer_spec(shape, layer):
    return pl.BlockSpec((1,) + tuple(shape), lambda *_: (layer, 0, 0), pipeline_mode=pl.Buffered(1))


def _cast_kernel(x_ref, o_ref):
    o_ref[...] = x_ref[...].astype(o_ref.dtype)


def _cast_bf16(w):
    depth, rows, cols = w.shape
    tr = CAST_ROWS
    return pl.pallas_call(
        _cast_kernel,
        out_shape=jax.ShapeDtypeStruct(w.shape, _BF),
        grid=(depth, rows // tr),
        in_specs=[pl.BlockSpec((1, tr, cols), lambda l, i: (l, i, 0))],
        out_specs=pl.BlockSpec((1, tr, cols), lambda l, i: (l, i, 0)),
        compiler_params=_params(2),
        name="cast_bf16",
    )(w)


def _dot(a, b):
    return jnp.dot(a, b, preferred_element_type=_F32)


def _dot_nt(a, b):
    return lax.dot_general(a, b, (((1,), (1,)), ((), ())), preferred_element_type=_F32)


def _exp2_bf(d):
    return jnp.exp2(d.astype(_BF))


def _swap_lanes(x, m):
    lane = lax.broadcasted_iota(jnp.int32, x.shape, 1)
    low = (lane % (2 * m)) < m
    return jnp.where(low, pltpu.roll(x, LANES - m, axis=1), pltpu.roll(x, m, axis=1))


def _norm_mod(x, g, shift, scale):
    y = x * lax.rsqrt(jnp.mean(x * x, axis=-1, keepdims=True) + EPS) * g
    return y * (1.0 + scale) + shift


def _mod_kernel(c_ref, w_ref, b_ref, o_ref):
    c = c_ref[...]
    a = c * jax.nn.sigmoid(c)
    o_ref[0] = jnp.dot(a, w_ref[0], preferred_element_type=_F32,
                       precision=lax.Precision.HIGHEST) + b_ref[0]


def _modulation(cs, w_mod, b_mod):
    depth, d, n = w_mod.shape
    tn = 1024
    rows = cs.shape[0]
    return pl.pallas_call(
        _mod_kernel,
        out_shape=jax.ShapeDtypeStruct((depth, rows, n), _F32),
        grid=(depth, n // tn),
        in_specs=[pl.BlockSpec((rows, d), lambda l, j: (0, 0)),
                  pl.BlockSpec((1, d, tn), lambda l, j: (l, 0, j)),
                  pl.BlockSpec((1, 1, tn), lambda l, j: (l, 0, j))],
        out_specs=pl.BlockSpec((1, rows, tn), lambda l, j: (l, 0, j)),
        compiler_params=_params(2),
        name="modulation",
    )(cs, w_mod, b_mod.reshape(depth, 1, n))


def _ffn_kernel(h_ref, mod_ref, g_ref, wg_ref, wu_ref, wd_ref, *rest, which, final, n_sub):
    if final:
        gf_ref, o_ref, a_ref = rest
    else:
        o_ref, a_ref = rest
    shift = mod_ref[0, 3 * which:3 * which + 1, :]
    scale = mod_ref[0, 3 * which + 1:3 * which + 2, :]
    gate = mod_ref[0, 3 * which + 2:3 * which + 3, :]
    d_ff = wg_ref.shape[2]
    rows_per = h_ref.shape[0] // n_sub
    for sub in range(n_sub):
        rows = slice(sub * rows_per, (sub + 1) * rows_per)
        x = h_ref[rows, :]
        xb = _norm_mod(x, g_ref[...], shift, scale).astype(_BF)
        for j in range(d_ff // FF_CHUNK):
            sl = slice(j * FF_CHUNK, (j + 1) * FF_CHUNK)
            gt = _dot(xb, wg_ref[0, :, sl])
            up = _dot(xb, wu_ref[0, :, sl])
            a_ref[rows, sl] = (gt * jax.nn.sigmoid(gt) * up).astype(_BF)
        y = x + 0.5 * gate * _dot(a_ref[rows, :], wd_ref[0])
        if final:
            y = y * lax.rsqrt(jnp.mean(y * y, axis=-1, keepdims=True) + EPS) * gf_ref[...]
        o_ref[rows, :] = y


def _ffn(h, mod, g, wg, wu, wd, *, layer, which, tm, tiles_per_mod, mod_base, g_final=None, n_sub=1):
    t, d = h.shape
    d_ff = wg.shape[2]
    final = g_final is not None
    tm = tm * n_sub
    if tiles_per_mod is None:
        mod_map = lambda i: (mod_base, 0, 0)
    else:
        mod_map = lambda i: (i // (tiles_per_mod // n_sub), 0, 0)
    in_specs = [pl.BlockSpec((tm, d), lambda i: (i, 0)),
                pl.BlockSpec((1, N_MOD, d), mod_map),
                _const_spec((1, d)),
                _layer_spec((d, d_ff), layer), _layer_spec((d, d_ff), layer), _layer_spec((d_ff, d), layer)]
    args = [h, mod, g, wg, wu, wd]
    if final:
        in_specs.append(_const_spec((1, d)))
        args.append(g_final)
    return pl.pallas_call(
        functools.partial(_ffn_kernel, which=which, final=final, n_sub=n_sub),
        out_shape=jax.ShapeDtypeStruct((t, d), _F32),
        grid=(t // tm,),
        in_specs=in_specs,
        out_specs=pl.BlockSpec((tm, d), lambda i: (i, 0)),
        scratch_shapes=[pltpu.VMEM((tm, d_ff), _BF)],
        compiler_params=_params(1),
        name="ffn",
    )(*args)


def _in_proj_kernel(h_ref, mod_ref, g_ref, win_ref, gq_ref, wuq_ref, gkv_ref, wukv_ref, *rest, rope):
    if rope:
        ca_ref, sa_ref, cw_ref, sw_ref = rest[:4]
        rest = rest[4:]
    qa_ref, ka_ref, vat_ref, bg_ref, cx_ref, qw_ref, kw_ref, vwt_ref, ud_ref, z_ref = rest

    x = h_ref[...]
    xb = _norm_mod(x, g_ref[...], mod_ref[0, 3:4, :], mod_ref[0, 4:5, :]).astype(_BF)

    z_ref[...] = _dot(xb, win_ref[...])

    def seg(lo, width):
        return z_ref[:, lo:lo + width]

    cq = seg(SEG_CQ, 256)
    cqn = cq * lax.rsqrt(jnp.sum(cq * cq, axis=-1, keepdims=True) * (1.0 / MLA_Q_RANK) + EPS) * gq_ref[...]
    qq = _dot(cqn.astype(_BF), wuq_ref[...])
    q = qq[:, :MLA_HEADS * LANES]
    if rope:
        ca = ca_ref[...]
        sa = sa_ref[...]
        q = (q * jnp.concatenate([ca] * MLA_HEADS, axis=1)
             + qq[:, MLA_HEADS * LANES:] * jnp.concatenate([sa] * MLA_HEADS, axis=1))
    qa_ref[...] = q.astype(_BF)

    krc = seg(SEG_KRC, 256)
    ckv = krc[:, LANES:]
    ckvn = ckv * lax.rsqrt(jnp.mean(ckv * ckv, axis=-1, keepdims=True) + EPS) * gkv_ref[...]
    kv = _dot(ckvn.astype(_BF), wukv_ref[...])
    kr = krc[:, :LANES]
    if rope:
        kr = kr * ca + _swap_lanes(kr, MLA_ROPE // 4) * sa
    ka_ref[...] = (kv[:, :MLA_HEADS * LANES] + jnp.concatenate([kr] * MLA_HEADS, axis=1)).astype(_BF)
    vat_ref[...] = kv[:, MLA_HEADS * LANES:].T.astype(_BF)

    sc = seg(SEG_SC, 768)
    bg_ref[...] = sc[:, :256]
    cx_ref[...] = sc[:, 256:512] * sc[:, 512:]

    qw = seg(SEG_WQ, 256)
    wkv = seg(SEG_WKV, 256)
    kw = wkv[:, :LANES]
    if rope:
        cw = cw_ref[...]
        sw = sw_ref[...]
        m_w = WA_HEAD_DIM // 4
        qw = jnp.concatenate([qw[:, :LANES] * cw + _swap_lanes(qw[:, :LANES], m_w) * sw,
                              qw[:, LANES:] * cw + _swap_lanes(qw[:, LANES:], m_w) * sw], axis=1)
        kw = kw * cw + _swap_lanes(kw, m_w) * sw
    qw_ref[...] = qw.astype(_BF)
    lane = lax.broadcasted_iota(jnp.int32, kw.shape, 1)
    low = lane < WA_HEAD_DIM
    kw_r = pltpu.roll(kw, WA_HEAD_DIM, axis=1)
    zero = jnp.zeros_like(kw)
    kw_ref[...] = jnp.concatenate(
        [jnp.where(low, kw, zero), jnp.where(low, zero, kw_r),
         jnp.where(low, kw_r, zero), jnp.where(low, zero, kw)], axis=1).astype(_BF)
    vwt_ref[...] = wkv[:, LANES:].T.astype(_BF)

    cf = seg(SEG_CF, 512)
    ud_ref[...] = cf[:, :256] * jax.nn.sigmoid(cf[:, 256:])


def _in_proj(h, mod, g, w, tabs, *, tm, tiles_per_mod, mod_base, seq_tiles):
    t, d = h.shape
    rope = tabs is not None
    if tiles_per_mod is None:
        mod_map = lambda i: (mod_base, 0, 0)
    else:
        mod_map = lambda i: (i // tiles_per_mod, 0, 0)
    in_specs = [pl.BlockSpec((tm, d), lambda i: (i, 0)),
                pl.BlockSpec((1, N_MOD, d), mod_map),
                _const_spec((1, d)),
                _const_spec((d, N_PACK)),
                _const_spec((1, 256)), _const_spec((256, 1024)),
                _const_spec((1, 128)), _const_spec((128, 768))]
    args = [h, mod, g, w["w_in"], w["g_q"], w["w_uq"], w["g_kv"], w["w_ukv"]]
    if rope:
        in_specs += [pl.BlockSpec((tm, LANES), lambda i: (i % seq_tiles, 0))] * 4
        args += list(tabs)
    row = lambda width: pl.BlockSpec((tm, width), lambda i: (i, 0))
    col = lambda height: pl.BlockSpec((height, tm), lambda i: (0, i))
    out_shape = (
        jax.ShapeDtypeStruct((t, 512), _BF),
        jax.ShapeDtypeStruct((t, 512), _BF),
        jax.ShapeDtypeStruct((256, t), _BF),
        jax.ShapeDtypeStruct((t, 256), _F32),
        jax.ShapeDtypeStruct((t, 256), _F32),
        jax.ShapeDtypeStruct((t, 256), _BF),
        jax.ShapeDtypeStruct((t, 512), _BF),
        jax.ShapeDtypeStruct((128, t), _BF),
        jax.ShapeDtypeStruct((t, 256), _F32),
    )
    out_specs = (row(512), row(512), col(256), row(256), row(256), row(256), row(512), col(128), row(256))
    return pl.pallas_call(
        functools.partial(_in_proj_kernel, rope=rope),
        out_shape=out_shape,
        grid=(t // tm,),
        in_specs=in_specs,
        out_specs=out_specs,
        scratch_shapes=[pltpu.VMEM((tm, N_PACK), _F32)],
        compiler_params=_params(1),
        name="in_proj",
    )(*args)


def _mla_kernel(q_ref, *refs, has_lat, n_lat_tiles):
    if has_lat:
        kl_ref, vl_ref, kc_ref, vc_ref, o_ref, s_ref, p_ref = refs
    else:
        kc_ref, vc_ref, o_ref, s_ref, p_ref = refs
    tq = q_ref.shape[0]
    n_ctx = kc_ref.shape[0]

    def scores_stage(k_ref, key_slice, tk):
        tile_max = []
        for h in range(MLA_HEADS):
            s = _dot_nt(k_ref[key_slice, h * LANES:(h + 1) * LANES],
                        q_ref[:, h * LANES:(h + 1) * LANES])
            s_ref[h, 0:tk, :] = s
            tile_max.append(jnp.max(s, axis=0, keepdims=True))
        return tuple(tile_max)

    def softmax_stage(tk, tile_max, ms):
        new_ms, alphas = [], []
        for h in range(MLA_HEADS):
            m_new = jnp.maximum(ms[h], tile_max[h])
            new_ms.append(m_new)
            alphas.append(jnp.exp2(ms[h] - m_new))
            for r0 in range(0, tk, SM_ROWS):
                p_ref[h, r0:r0 + SM_ROWS, :] = _exp2_bf(s_ref[h, r0:r0 + SM_ROWS, :] - m_new)
        return tuple(new_ms), tuple(alphas)

    def value_stage(vt_ref, key_slice, tk, alphas, accs):
        ones = jnp.ones((ONES_ROWS, tk), _BF)
        new = []
        for h in range(MLA_HEADS):
            v1 = jnp.concatenate([vt_ref[h * MLA_V:(h + 1) * MLA_V, key_slice], ones], axis=0)
            new.append(alphas[h] * accs[h] + _dot(v1, p_ref[h, 0:tk, :]))
        return tuple(new)

    ms = tuple(jnp.full((1, tq), NEG, _F32) for _ in range(MLA_HEADS))
    accs = tuple(jnp.zeros((MLA_V + ONES_ROWS, tq), _F32) for _ in range(MLA_HEADS))
    if has_lat:
        def lat(t):
            return pl.ds(pl.multiple_of(t * TK_MLA, TK_MLA), TK_MLA)

        def body(k, state, ctx_next):
            tile_max, ms, accs = state
            ms, alphas = softmax_stage(TK_MLA, tile_max, ms)
            if ctx_next:
                tile_max = scores_stage(kc_ref, slice(None), n_ctx)
            else:
                tile_max = scores_stage(kl_ref, lat(k + 1), TK_MLA)
            return tile_max, ms, value_stage(vl_ref, lat(k), TK_MLA, alphas, accs)

        state = (scores_stage(kl_ref, lat(0), TK_MLA), ms, accs)
        state = lax.fori_loop(0, n_lat_tiles - 1, functools.partial(body, ctx_next=False), state)
        tile_max, ms, accs = body(n_lat_tiles - 1, state, True)
    else:
        tile_max = scores_stage(kc_ref, slice(None), n_ctx)
    ms, alphas = softmax_stage(n_ctx, tile_max, ms)
    accs = value_stage(vc_ref, slice(None), n_ctx, alphas, accs)
    outs = [acc[:MLA_V] / acc[MLA_V:MLA_V + 1] for acc in accs]
    o_ref[...] = jnp.concatenate(outs, axis=0).T.astype(_BF)


def _mla_attn(q, k_lat, vt_lat, k_ctx, vt_ctx, *, batch, tq):
    t = q.shape[0]
    per_b = t // batch
    nq = per_b // tq
    c = k_ctx.shape[0] // batch
    has_lat = k_lat is not None
    in_specs = [pl.BlockSpec((tq, 512), lambda b, j: (b * nq + j, 0))]
    args = [q]
    n_lat_tiles = 0
    if has_lat:
        s = k_lat.shape[0] // batch
        n_lat_tiles = s // TK_MLA
        in_specs += [pl.BlockSpec((s, 512), lambda b, j: (b, 0)),
                     pl.BlockSpec((256, s), lambda b, j: (0, b))]
        args += [k_lat, vt_lat]
    in_specs += [pl.BlockSpec((c, 512), lambda b, j: (b, 0)),
                 pl.BlockSpec((256, c), lambda b, j: (0, b))]
    args += [k_ctx, vt_ctx]
    return pl.pallas_call(
        functools.partial(_mla_kernel, has_lat=has_lat, n_lat_tiles=n_lat_tiles),
        out_shape=jax.ShapeDtypeStruct((t, 256), _BF),
        grid=(batch, nq),
        in_specs=in_specs,
        out_specs=pl.BlockSpec((tq, 256), lambda b, j: (b * nq + j, 0)),
        scratch_shapes=[pltpu.VMEM((MLA_HEADS, max(TK_MLA, c), tq), _F32),
                        pltpu.VMEM((MLA_HEADS, max(TK_MLA, c), tq), _BF)],
        compiler_params=_params(2),
        name="mla_attn",
    )(*args)


def _wa_kernel(sink_ref, q_ref, *refs, has_lat, seq_len, n_sub):
    if has_lat:
        kl_ref, vl_ref, kc_ref, vc_ref, o_ref, sc_ref, pc_ref, sl_ref, pl_ref, bias_ref = refs
    else:
        kc_ref, vc_ref, o_ref, sc_ref, pc_ref = refs
    tq = q_ref.shape[0] // n_sub
    n_ctx = kc_ref.shape[0]
    win_keys = tq + 2 * WINDOW
    units = [(sub, hq) for sub in range(n_sub) for hq in range(WA_HEADS)]
    local = []
    if has_lat:
        for sub in range(n_sub):
            q0 = (pl.program_id(1) * n_sub + sub) * tq
            start = pl.multiple_of(jnp.clip(q0 - WINDOW, 0, seq_len - win_keys), LANES)
            local.append(pl.ds(start, win_keys))
            kpos = start + lax.broadcasted_iota(jnp.int32, (win_keys, tq), 0)
            qpos = q0 + lax.broadcasted_iota(jnp.int32, (win_keys, tq), 1)
            bias_ref[sub] = jnp.where(jnp.abs(kpos - qpos) <= WINDOW, 0.0, NEG)

    stats = {}
    for sub, hq in units:
        g = hq // (WA_HEADS // WA_KV_HEADS)
        lanes = slice(hq * LANES, (hq + 1) * LANES)
        qpair = q_ref[sub * tq:(sub + 1) * tq, g * LANES:(g + 1) * LANES]
        snk = sink_ref[hq] * LOG2E
        s_ctx = _dot_nt(kc_ref[:, lanes], qpair)
        sc_ref[sub, hq] = s_ctx
        m = jnp.maximum(jnp.max(s_ctx, axis=0, keepdims=True), snk)
        if has_lat:
            s_loc = _dot_nt(kl_ref[local[sub], lanes], qpair) + bias_ref[sub]
            sl_ref[sub, hq] = s_loc
            m = jnp.maximum(m, jnp.max(s_loc, axis=0, keepdims=True))
        stats[sub, hq] = (m, jnp.exp2(snk - m))

    for sub, hq in units:
        m = stats[sub, hq][0]
        for r0 in range(0, n_ctx, SM_ROWS):
            pc_ref[sub, hq, r0:r0 + SM_ROWS, :] = _exp2_bf(sc_ref[sub, hq, r0:r0 + SM_ROWS, :] - m)
        if has_lat:
            for r0 in range(0, win_keys, SM_ROWS):
                pl_ref[sub, hq, r0:r0 + SM_ROWS, :] = _exp2_bf(sl_ref[sub, hq, r0:r0 + SM_ROWS, :] - m)

    for sub in range(n_sub):
        outs = []
        for hq in range(WA_HEADS):
            g = hq // (WA_HEADS // WA_KV_HEADS)
            rows = slice(g * WA_HEAD_DIM, (g + 1) * WA_HEAD_DIM)
            acc = _dot(jnp.concatenate([vc_ref[rows, :], jnp.ones((ONES_ROWS, n_ctx), _BF)], axis=0),
                       pc_ref[sub, hq])
            if has_lat:
                acc = acc + _dot(jnp.concatenate([vl_ref[rows, local[sub]],
                                                  jnp.ones((ONES_ROWS, win_keys), _BF)], axis=0), pl_ref[sub, hq])
            l = acc[WA_HEAD_DIM:WA_HEAD_DIM + 1] + stats[sub, hq][1]
            outs.append(acc[:WA_HEAD_DIM] / l)
        o_ref[sub * tq:(sub + 1) * tq, :] = jnp.concatenate(outs, axis=0).T.astype(_BF)


def _wa_attn(sink, q, k_lat, vt_lat, k_ctx, vt_ctx, *, batch, tq, n_sub):
    t = q.shape[0]
    per_b = t // batch
    step_q = tq * n_sub
    nq = per_b // step_q
    c = k_ctx.shape[0] // batch
    has_lat = k_lat is not None
    in_specs = [pl.BlockSpec(memory_space=pltpu.SMEM),
                pl.BlockSpec((step_q, 256), lambda b, j: (b * nq + j, 0))]
    args = [sink, q]
    seq_len = 0
    if has_lat:
        seq_len = k_lat.shape[0] // batch
        in_specs += [pl.BlockSpec((seq_len, 512), lambda b, j: (b, 0)),
                     pl.BlockSpec((128, seq_len), lambda b, j: (0, b))]
        args += [k_lat, vt_lat]
    in_specs += [pl.BlockSpec((c, 512), lambda b, j: (b, 0)),
                 pl.BlockSpec((128, c), lambda b, j: (0, b))]
    args += [k_ctx, vt_ctx]
    win_keys = tq + 2 * WINDOW
    scratch = [pltpu.VMEM((n_sub, WA_HEADS, c, tq), _F32), pltpu.VMEM((n_sub, WA_HEADS, c, tq), _BF)]
    if has_lat:
        scratch += [pltpu.VMEM((n_sub, WA_HEADS, win_keys, tq), _F32),
                    pltpu.VMEM((n_sub, WA_HEADS, win_keys, tq), _BF),
                    pltpu.VMEM((n_sub, win_keys, tq), _F32)]
    return pl.pallas_call(
        functools.partial(_wa_kernel, has_lat=has_lat, seq_len=seq_len, n_sub=n_sub),
        out_shape=jax.ShapeDtypeStruct((t, 256), _BF),
        grid=(batch, nq),
        in_specs=in_specs,
        out_specs=pl.BlockSpec((step_q, 256), lambda b, j: (b * nq + j, 0)),
        scratch_shapes=scratch,
        compiler_params=_params(2),
        name="wa_attn",
    )(*args)


HALO_SC = 8
HALO_CF = 16
CONV_ROWS = 64
SC_PHASES = ((HALO_SC - SC_K // 2) % SUBLANES, (HALO_SC + SC_K // 2) % SUBLANES)
CF_PHASES = tuple(range(1, SUBLANES))


def _mix_out_kernel(h_ref, mod_ref, oa_ref, oc_ref, bg_ref, cx_ref, ud_ref, *rest, halo, seq_tiles):
    if halo:
        cxp_ref, cxn_ref, udp_ref, udn_ref = rest[:4]
        rest = rest[4:]
    (wsc_ref, wcf_ref, bcf_ref, gln_ref, bln_ref, wout_ref, o_ref,
     xsc_ref, xcf_ref, ssc_ref, scf_ref, ob_ref, od_ref) = rest
    tm = h_ref.shape[0]
    width = cx_ref.shape[1]

    if halo:
        i = pl.program_id(0) % seq_tiles
        has_prev = (i != 0).astype(_F32)
        has_next = (i != seq_tiles - 1).astype(_F32)
        xsc_ref[0:HALO_SC, :] = cxp_ref[...] * has_prev
        xsc_ref[HALO_SC + tm:, :] = cxn_ref[...] * has_next
        xcf_ref[0:HALO_CF, :] = udp_ref[...] * has_prev
        xcf_ref[HALO_CF + tm:, :] = udn_ref[...] * has_next
    else:
        xsc_ref[0:HALO_SC, :] = jnp.zeros((HALO_SC, width), _F32)
        xsc_ref[HALO_SC + tm:, :] = jnp.zeros((HALO_SC, width), _F32)
        xcf_ref[0:HALO_CF, :] = jnp.zeros((HALO_CF, width), _F32)
        xcf_ref[HALO_CF + tm:, :] = jnp.zeros((HALO_CF, width), _F32)
    xsc_ref[HALO_SC:HALO_SC + tm, :] = cx_ref[...]
    xcf_ref[HALO_CF:HALO_CF + tm, :] = ud_ref[...]

    for r in range(1, SUBLANES):
        scf_ref[r - 1] = xcf_ref[r:r + scf_ref.shape[1], :]
    for n, r in enumerate(SC_PHASES):
        ssc_ref[n] = xsc_ref[r:r + ssc_ref.shape[1], :]

    def tap(x_ref, copies_ref, phases, row):
        r = row % SUBLANES
        base = row - r
        if r == 0:
            return x_ref[base:base + CONV_ROWS, :]
        return copies_ref[phases.index(r), base:base + CONV_ROWS, :]

    for r0 in range(0, tm, CONV_ROWS):
        acc = jnp.zeros((CONV_ROWS, width), _F32)
        for k in range(SC_K):
            row = HALO_SC + r0 + k - SC_K // 2
            acc = acc + tap(xsc_ref, ssc_ref, SC_PHASES, row) * wsc_ref[k:k + 1, :]
        ob_ref[r0:r0 + CONV_ROWS, :] = (bg_ref[r0:r0 + CONV_ROWS, :] * acc).astype(_BF)

        acc = jnp.zeros((CONV_ROWS, width), _F32)
        for k in range(CF_K):
            row = HALO_CF + r0 + k - CF_K // 2
            acc = acc + tap(xcf_ref, scf_ref, CF_PHASES, row) * wcf_ref[k:k + 1, :]
        u = acc + bcf_ref[...]
        mu = jnp.mean(u, axis=-1, keepdims=True)
        uc = u - mu
        var = jnp.mean(uc * uc, axis=-1, keepdims=True)
        y = uc * lax.rsqrt(var + EPS) * gln_ref[...] + bln_ref[...]
        od_ref[r0:r0 + CONV_ROWS, :] = (y * jax.nn.sigmoid(y)).astype(_BF)

    mixed = jnp.concatenate([oa_ref[...], ob_ref[...], oc_ref[...], od_ref[...]], axis=1)
    o_ref[...] = h_ref[...] + mod_ref[0, 5:6, :] * _dot(mixed, wout_ref[0])


def _mix_out(h, mod, oa, oc, bg, cx, ud, w, w_out, *, layer, tm, tiles_per_mod, mod_base, seq_tiles):
    t, d = h.shape
    halo = seq_tiles > 1
    if tiles_per_mod is None:
        mod_map = lambda i: (mod_base, 0, 0)
    else:
        mod_map = lambda i: (i // tiles_per_mod, 0, 0)
    row = lambda width: pl.BlockSpec((tm, width), lambda i: (i, 0))
    in_specs = [row(d), pl.BlockSpec((1, N_MOD, d), mod_map), row(256), row(256), row(256), row(256), row(256)]
    args = [h, mod, oa, oc, bg, cx, ud]
    if halo:
        nsc = tm // HALO_SC
        ncf = tm // HALO_CF
        last_sc = t // HALO_SC - 1
        last_cf = t // HALO_CF - 1
        in_specs += [
            pl.BlockSpec((HALO_SC, 256), lambda i: (jnp.maximum(i * nsc - 1, 0), 0)),
            pl.BlockSpec((HALO_SC, 256), lambda i: (jnp.minimum((i + 1) * nsc, last_sc), 0)),
            pl.BlockSpec((HALO_CF, 256), lambda i: (jnp.maximum(i * ncf - 1, 0), 0)),
            pl.BlockSpec((HALO_CF, 256), lambda i: (jnp.minimum((i + 1) * ncf, last_cf), 0)),
        ]
        args += [cx, cx, ud, ud]
    in_specs += [_const_spec((SC_K, 256)), _const_spec((CF_K, 256)), _const_spec((1, 256)),
                 _const_spec((1, 256)), _const_spec((1, 256)), _layer_spec((d, d), layer)]
    args += [w["w_sc"], w["w_cf"], w["b_cf"], w["g_ln"], w["b_ln"], w_out]
    return pl.pallas_call(
        functools.partial(_mix_out_kernel, halo=halo, seq_tiles=seq_tiles),
        out_shape=jax.ShapeDtypeStruct((t, d), _F32),
        grid=(t // tm,),
        in_specs=in_specs,
        out_specs=row(d),
        scratch_shapes=[pltpu.VMEM((tm + 2 * HALO_SC, 256), _F32),
                        pltpu.VMEM((tm + 2 * HALO_CF, 256), _F32),
                        pltpu.VMEM((len(SC_PHASES), tm + HALO_SC, 256), _F32),
                        pltpu.VMEM((len(CF_PHASES), tm + 2 * HALO_CF - SUBLANES, 256), _F32),
                        pltpu.VMEM((tm, 256), _BF),
                        pltpu.VMEM((tm, 256), _BF)],
        compiler_params=_params(1),
        name="mix_out",
    )(*args)


def _swap_halves(w, d_rot):
    m = d_rot // 4
    lead = w.shape[:-1]
    return jnp.flip(w.reshape(lead + (-1, 2, m)), axis=-2).reshape(w.shape)


def _rope_table(seq, d_rot, lanes_before):
    m = d_rot // 4
    d_ax = d_rot // 2
    rows = seq // GRID_W
    inv = ROPE_BASE ** (-jnp.arange(0, d_ax, 2, dtype=_F32) / d_ax)
    ar = jnp.arange(rows, dtype=_F32)[:, None] * inv[None, :]
    ac = jnp.arange(GRID_W, dtype=_F32)[:, None] * inv[None, :]

    def per_row(v):
        return jnp.broadcast_to(v[:, None, :], (rows, GRID_W, m)).reshape(seq, m)

    def per_col(v):
        return jnp.broadcast_to(v[None, :, :], (rows, GRID_W, m)).reshape(seq, m)

    cr, sr, cc, sc = per_row(jnp.cos(ar)), per_row(jnp.sin(ar)), per_col(jnp.cos(ac)), per_col(jnp.sin(ac))
    cos = jnp.concatenate([cr, cr, cc, cc], axis=1)
    sin = jnp.concatenate([-sr, sr, -sc, sc], axis=1)
    reps = (LANES - lanes_before) // d_rot if lanes_before == 0 else 1
    pad = LANES - lanes_before - reps * d_rot
    c_tab = jnp.concatenate([jnp.ones((seq, lanes_before), _F32)] + [cos] * reps + [jnp.ones((seq, pad), _F32)], axis=1)
    s_tab = jnp.concatenate([jnp.zeros((seq, lanes_before), _F32)] + [sin] * reps + [jnp.zeros((seq, pad), _F32)], axis=1)
    return c_tab, s_tab


def _pack_layer(p, l):
    offs = np.concatenate([[0], np.cumsum(IN_SIZES)])
    o_cq, o_ckv, o_kr, o_sc, o_wq, o_wkv, o_cf, o_end = (int(v) for v in offs)
    scale_a = float((MLA_NOPE + MLA_ROPE) ** -0.5 * LOG2E)
    scale_w = float(WA_HEAD_DIM ** -0.5 * LOG2E)
    w = p["w_in"][l]
    d = w.shape[0]
    zeros = lambda n: jnp.zeros((d, n), _F32)

    kr = w[:, o_kr:o_sc]
    wq = w[:, o_wq:o_wkv] * scale_w
    nk = WA_KV_HEADS * WA_HEAD_DIM
    wk = w[:, o_wkv:o_wkv + nk]
    pad_r = LANES - MLA_NOPE - MLA_ROPE
    w_in = jnp.concatenate([
        w[:, o_cq:o_ckv], zeros(256 - MLA_Q_RANK),
        zeros(MLA_NOPE), kr, zeros(pad_r), w[:, o_ckv:o_kr],
        w[:, o_sc:o_wq],
        wq,
        w[:, o_cf:o_end],
        wk, w[:, o_wkv + nk:o_cf],
    ], axis=1).astype(_BF)

    wu = p["w_mla_uq"][l].reshape(MLA_Q_RANK, MLA_HEADS, MLA_NOPE + MLA_ROPE) * scale_a
    nope, rope = wu[..., :MLA_NOPE], wu[..., MLA_NOPE:]
    zq = lambda n: jnp.zeros((MLA_Q_RANK, MLA_HEADS, n), _F32)
    w_uq = jnp.concatenate([
        jnp.concatenate([nope, rope, zq(pad_r)], axis=-1).reshape(MLA_Q_RANK, MLA_HEADS * LANES),
        jnp.concatenate([zq(MLA_NOPE), _swap_halves(rope, MLA_ROPE), zq(pad_r)],
                        axis=-1).reshape(MLA_Q_RANK, MLA_HEADS * LANES),
    ], axis=1)
    w_uq = jnp.pad(w_uq, ((0, 256 - MLA_Q_RANK), (0, 0))).astype(_BF)
    g_q = jnp.pad(p["g_mla_q"][l], (0, 256 - MLA_Q_RANK)).reshape(1, 256)

    wkv = p["w_mla_ukv"][l].reshape(MLA_KV_RANK, MLA_HEADS, MLA_NOPE + MLA_V)
    k_nope = jnp.concatenate(
        [wkv[..., :MLA_NOPE], jnp.zeros((MLA_KV_RANK, MLA_HEADS, LANES - MLA_NOPE), _F32)], axis=-1)
    w_ukv = jnp.concatenate([k_nope.reshape(MLA_KV_RANK, MLA_HEADS * LANES),
                             wkv[..., MLA_NOPE:].reshape(MLA_KV_RANK, MLA_HEADS * MLA_V)], axis=1).astype(_BF)

    return dict(
        w_in=w_in, w_uq=w_uq, g_q=g_q, w_ukv=w_ukv, g_kv=p["g_mla_kv"][l].reshape(1, MLA_KV_RANK),
        w_sc=p["w_sc_conv"][l], w_cf=p["w_cf_conv"][l], b_cf=p["b_cf_conv"][l].reshape(1, -1),
        g_ln=p["g_cf_ln"][l].reshape(1, -1), b_ln=p["b_cf_ln"][l].reshape(1, -1),
        g_ffn1=p["g_ffn1"][l].reshape(1, -1), g_mix=p["g_mix"][l].reshape(1, -1),
        g_ffn2=p["g_ffn2"][l].reshape(1, -1),
        sink=p["wa_sink"][l],
    )


def kernel(x, c, ctx, c_ctx, w_mod, b_mod, g_ffn1, w1_gate, w1_up, w1_down, g_mix, w_in, g_mla_q, w_mla_uq, g_mla_kv, w_mla_ukv, w_sc_conv, wa_sink, w_cf_conv, b_cf_conv, g_cf_ln, b_cf_ln, w_out, g_ffn2, w2_gate, w2_up, w2_down, g_final):
    b, s, d = x.shape
    n_ctx = ctx.shape[1]
    depth = w_mod.shape[0]
    p = dict(w_in=w_in, g_mla_q=g_mla_q, w_mla_uq=w_mla_uq, g_mla_kv=g_mla_kv, w_mla_ukv=w_mla_ukv,
             w_sc_conv=w_sc_conv, wa_sink=wa_sink, w_cf_conv=w_cf_conv, b_cf_conv=b_cf_conv,
             g_cf_ln=g_cf_ln, b_cf_ln=b_cf_ln, w_out=w_out, g_ffn1=g_ffn1, g_mix=g_mix, g_ffn2=g_ffn2,
             w1_gate=w1_gate, w1_up=w1_up, w1_down=w1_down, w2_gate=w2_gate, w2_up=w2_up, w2_down=w2_down)

    tm = TM_LAT
    seq_tiles = s // tm
    tabs = _rope_table(s, MLA_ROPE, MLA_NOPE) + _rope_table(s, WA_HEAD_DIM, 0)

    mod_rows = 8
    cs = jnp.concatenate([c, c_ctx[None, :], jnp.zeros((mod_rows - b - 1, d), _F32)], axis=0)
    mod_all = _modulation(cs, w_mod, b_mod).reshape(depth, mod_rows, N_MOD, d)

    lat = dict(tm=tm, tiles_per_mod=seq_tiles, mod_base=0)
    cxt = dict(tm=n_ctx, tiles_per_mod=None, mod_base=b)

    w1 = (_cast_bf16(w1_gate), _cast_bf16(w1_up), _cast_bf16(w1_down))
    w2 = (_cast_bf16(w2_gate), _cast_bf16(w2_up), _cast_bf16(w2_down))
    w_out_b = _cast_bf16(w_out)

    h_lat = x.reshape(b * s, d)
    h_ctx = ctx.reshape(b * n_ctx, d)
    for l in range(depth):
        last = l == depth - 1
        w = _pack_layer(p, l)
        mod = mod_all[l]

        h_lat = _ffn(h_lat, mod, w["g_ffn1"], *w1, layer=l, which=0, n_sub=FFN_SUBTILES, **lat)
        h_ctx = _ffn(h_ctx, mod, w["g_ffn1"], *w1, layer=l, which=0, **cxt)

        qa, ka, vat, bg, cx, qw, kw, vwt, ud = _in_proj(h_lat, mod, w["g_mix"], w, tabs, seq_tiles=seq_tiles, **lat)
        qa_c, ka_c, vat_c, bg_c, cx_c, qw_c, kw_c, vwt_c, ud_c = _in_proj(
            h_ctx, mod, w["g_mix"], w, None, seq_tiles=1, **cxt)

        oa = _mla_attn(qa, ka, vat, ka_c, vat_c, batch=b, tq=TQ_MLA)
        oc = _wa_attn(w["sink"], qw, kw, vwt, kw_c, vwt_c, batch=b, tq=TQ_WA, n_sub=WA_SUBTILES)
        h_lat = _mix_out(h_lat, mod, oa, oc, bg, cx, ud, w, w_out_b, layer=l, seq_tiles=seq_tiles, **lat)

        if not last:
            oa_c = _mla_attn(qa_c, None, None, ka_c, vat_c, batch=b, tq=n_ctx)
            oc_c = _wa_attn(w["sink"], qw_c, None, None, kw_c, vwt_c, batch=b, tq=n_ctx, n_sub=1)
            h_ctx = _mix_out(h_ctx, mod, oa_c, oc_c, bg_c, cx_c, ud_c, w, w_out_b, layer=l, seq_tiles=1, **cxt)
            h_ctx = _ffn(h_ctx, mod, w["g_ffn2"], *w2, layer=l, which=2, **cxt)

        h_lat = _ffn(h_lat, mod, w["g_ffn2"], *w2, layer=l, which=2,
                     g_final=g_final.reshape(1, d) if last else None, n_sub=FFN_SUBTILES, **lat)

    return h_lat.reshape(b, s, d)
```

```python
import functools

import numpy as np
import jax
import jax.numpy as jnp
from jax import lax
from jax.experimental import pallas as pl
from jax.experimental.pallas import tpu as pltpu

GRID_W = 64
ROPE_BASE = 10000.0
EPS = 1e-6
NEG = -1e30
N_MOD = 9
LOG2E = 1.4426950408889634

MLA_HEADS = 4
MLA_NOPE = 64
MLA_ROPE = 32
MLA_V = 64
MLA_Q_RANK = 192
MLA_KV_RANK = 128
SC_WIDTH = 256
SC_K = 3
WA_HEADS = 4
WA_KV_HEADS = 2
WA_HEAD_DIM = 64
WINDOW = 128
CF_WIDTH = 256
CF_K = 31
IN_SIZES = (MLA_Q_RANK, MLA_KV_RANK, MLA_ROPE, 3 * SC_WIDTH,
            WA_HEADS * WA_HEAD_DIM, 2 * WA_KV_HEADS * WA_HEAD_DIM, 2 * CF_WIDTH)

LANES = 128
SUBLANES = 8
MXU_N = 256
SM_ROWS = 32
ONES_ROWS = 16

_BF = jnp.bfloat16
_F32 = jnp.float32

SEG_CQ = 0
SEG_KRC = 256
SEG_SC = 512
SEG_WQ = 1280
SEG_CF = 1536
SEG_WKV = 2048
N_PACK = 2304

TM_LAT = 512
TQ_MLA = 512
TK_MLA = 1024
TQ_WA = 256
WA_SUBTILES = 2
FF_CHUNK = 256
CAST_BLOCK_BYTES = 12 * 1024 * 1024
FFN_SUBTILES = 2
VMEM_LIMIT = 52 * 1024 * 1024


def _params(n_axes, flags=None):
    return pltpu.CompilerParams(dimension_semantics=("arbitrary",) * n_axes,
                                vmem_limit_bytes=VMEM_LIMIT, flags=flags)


def _const_spec(shape):
    nd = len(shape)
    return pl.BlockSpec(shape, lambda *_: (0,) * nd, pipeline_mode=pl.Buffered(1))


def _layer_spec(shape, layer):
    return pl.BlockSpec((1,) + tuple(shape), lambda *_: (layer, 0, 0), pipeline_mode=pl.Buffered(1))


def _cast_kernel(x_ref, o_ref):
    o_ref[...] = x_ref[...].astype(o_ref.dtype)


def _cast_bf16(w):
    depth, rows, cols = w.shape
    tr = rows
    while tr * cols * 4 > CAST_BLOCK_BYTES:
        tr //= 2
    return pl.pallas_call(
        _cast_kernel,
        out_shape=jax.ShapeDtypeStruct(w.shape, _BF),
        grid=(depth, rows // tr),
        in_specs=[pl.BlockSpec((1, tr, cols), lambda l, i: (l, i, 0))],
        out_specs=pl.BlockSpec((1, tr, cols), lambda l, i: (l, i, 0)),
        compiler_params=_params(2),
        name="cast_bf16",
    )(w)


def _dot(a, b):
    return jnp.dot(a, b, preferred_element_type=_F32)


def _dot_nt(a, b):
    return lax.dot_general(a, b, (((1,), (1,)), ((), ())), preferred_element_type=_F32)


def _exp2_bf(d):
    return jnp.exp2(d).astype(_BF)


def _swap_lanes(x, m):
    lane = lax.broadcasted_iota(jnp.int32, x.shape, 1)
    low = (lane % (2 * m)) < m
    return jnp.where(low, pltpu.roll(x, LANES - m, axis=1), pltpu.roll(x, m, axis=1))


def _norm_mod(x, g, shift, scale):
    y = x * lax.rsqrt(jnp.mean(x * x, axis=-1, keepdims=True) + EPS) * g
    return y * (1.0 + scale) + shift


def _mod_kernel(c_ref, w_ref, b_ref, o_ref):
    c = c_ref[...]
    a = c * jax.nn.sigmoid(c)
    rows = a.shape[0]
    a_hi = a.astype(_BF).astype(_F32)
    lhs = jnp.concatenate([a_hi, a - a_hi], axis=0).astype(_BF)
    w = w_ref[0]
    w_hi = w.astype(_BF)
    w_lo = (w - w_hi.astype(_F32)).astype(_BF)
    r = _dot(lhs, w_hi) + _dot(lhs, w_lo)
    o_ref[0] = r[:rows] + r[rows:] + b_ref[0]


def _modulation(cs, w_mod, b_mod):
    depth, d, n = w_mod.shape
    tn = 1024
    rows = cs.shape[0]
    return pl.pallas_call(
        _mod_kernel,
        out_shape=jax.ShapeDtypeStruct((depth, rows, n), _F32),
        grid=(depth, n // tn),
        in_specs=[pl.BlockSpec((rows, d), lambda l, j: (0, 0)),
                  pl.BlockSpec((1, d, tn), lambda l, j: (l, 0, j)),
                  pl.BlockSpec((1, 1, tn), lambda l, j: (l, 0, j))],
        out_specs=pl.BlockSpec((1, rows, tn), lambda l, j: (l, 0, j)),
        compiler_params=_params(2),
        name="modulation",
    )(cs, w_mod, b_mod.reshape(depth, 1, n))


def _ffn_kernel(h_ref, mod_ref, g_ref, wg_ref, wu_ref, wd_ref, *rest, which, final, n_sub):
    if final:
        gf_ref, o_ref, a_ref = rest
    else:
        o_ref, a_ref = rest
    shift = mod_ref[0, 3 * which:3 * which + 1, :]
    scale = mod_ref[0, 3 * which + 1:3 * which + 2, :]
    gate = mod_ref[0, 3 * which + 2:3 * which + 3, :]
    d_ff = wg_ref.shape[2]
    rows_per = h_ref.shape[0] // n_sub
    for sub in range(n_sub):
        rows = slice(sub * rows_per, (sub + 1) * rows_per)
        x = h_ref[rows, :]
        xb = _norm_mod(x, g_ref[...], shift, scale).astype(_BF)
        for j in range(d_ff // FF_CHUNK):
            sl = slice(j * FF_CHUNK, (j + 1) * FF_CHUNK)
            gt = _dot(xb, wg_ref[0, :, sl])
            up = _dot(xb, wu_ref[0, :, sl])
            a_ref[rows, sl] = (gt * jax.nn.sigmoid(gt) * up).astype(_BF)
        y = x + 0.5 * gate * _dot(a_ref[rows, :], wd_ref[0])
        if final:
            y = y * lax.rsqrt(jnp.mean(y * y, axis=-1, keepdims=True) + EPS) * gf_ref[...]
        o_ref[rows, :] = y


def _ffn(h, mod, g, wg, wu, wd, *, layer, which, tm, tiles_per_mod, mod_base, g_final=None, n_sub=1):
    t, d = h.shape
    d_ff = wg.shape[2]
    final = g_final is not None
    tm = tm * n_sub
    if tiles_per_mod is None:
        mod_map = lambda i: (mod_base, 0, 0)
    else:
        mod_map = lambda i: (i // (tiles_per_mod // n_sub), 0, 0)
    in_specs = [pl.BlockSpec((tm, d), lambda i: (i, 0)),
                pl.BlockSpec((1, N_MOD, d), mod_map),
                _const_spec((1, d)),
                _layer_spec((d, d_ff), layer), _layer_spec((d, d_ff), layer), _layer_spec((d_ff, d), layer)]
    args = [h, mod, g, wg, wu, wd]
    if final:
        in_specs.append(_const_spec((1, d)))
        args.append(g_final)
    return pl.pallas_call(
        functools.partial(_ffn_kernel, which=which, final=final, n_sub=n_sub),
        out_shape=jax.ShapeDtypeStruct((t, d), _F32),
        grid=(t // tm,),
        in_specs=in_specs,
        out_specs=pl.BlockSpec((tm, d), lambda i: (i, 0)),
        scratch_shapes=[pltpu.VMEM((tm, d_ff), _BF)],
        compiler_params=_params(1),
        name="ffn",
    )(*args)


def _in_proj_kernel(h_ref, mod_ref, g_ref, win_ref, gq_ref, wuq_ref, gkv_ref, wukv_ref, *rest, rope):
    if rope:
        ca_ref, sa_ref, cw_ref, sw_ref = rest[:4]
        rest = rest[4:]
    qa_ref, ka_ref, vat_ref, bg_ref, cx_ref, qw_ref, kw_ref, vwt_ref, ud_ref, z_ref = rest

    x = h_ref[...]
    xb = _norm_mod(x, g_ref[...], mod_ref[0, 3:4, :], mod_ref[0, 4:5, :]).astype(_BF)

    z_ref[...] = _dot(xb, win_ref[...])

    def seg(lo, width):
        return z_ref[:, lo:lo + width]

    cq = seg(SEG_CQ, 256)
    cqn = cq * lax.rsqrt(jnp.sum(cq * cq, axis=-1, keepdims=True) * (1.0 / MLA_Q_RANK) + EPS) * gq_ref[...]
    qq = _dot(cqn.astype(_BF), wuq_ref[...])
    q = qq[:, :MLA_HEADS * LANES]
    if rope:
        ca = ca_ref[...]
        sa = sa_ref[...]
        q = (q * jnp.concatenate([ca] * MLA_HEADS, axis=1)
             + qq[:, MLA_HEADS * LANES:] * jnp.concatenate([sa] * MLA_HEADS, axis=1))
    qa_ref[...] = q.astype(_BF)

    krc = seg(SEG_KRC, 256)
    ckv = krc[:, LANES:]
    ckvn = ckv * lax.rsqrt(jnp.mean(ckv * ckv, axis=-1, keepdims=True) + EPS) * gkv_ref[...]
    kv = _dot(ckvn.astype(_BF), wukv_ref[...])
    kr = krc[:, :LANES]
    if rope:
        kr = kr * ca + _swap_lanes(kr, MLA_ROPE // 4) * sa
    for h in range(MLA_HEADS):
        ka_ref[h] = (kv[:, h * LANES:(h + 1) * LANES] + kr).astype(_BF)
    vat_ref[...] = kv[:, MLA_HEADS * LANES:].T.astype(_BF)

    sc = seg(SEG_SC, 768)
    bg_ref[...] = sc[:, :256]
    cx_ref[...] = sc[:, 256:512] * sc[:, 512:]

    qw = seg(SEG_WQ, 256)
    wkv = seg(SEG_WKV, 256)
    kw = wkv[:, :LANES]
    if rope:
        cw = cw_ref[...]
        sw = sw_ref[...]
        m_w = WA_HEAD_DIM // 4
        qw = jnp.concatenate([qw[:, :LANES] * cw + _swap_lanes(qw[:, :LANES], m_w) * sw,
                              qw[:, LANES:] * cw + _swap_lanes(qw[:, LANES:], m_w) * sw], axis=1)
        kw = kw * cw + _swap_lanes(kw, m_w) * sw
    qw_ref[...] = qw.astype(_BF)
    lane = lax.broadcasted_iota(jnp.int32, kw.shape, 1)
    low = lane < WA_HEAD_DIM
    kw_r = pltpu.roll(kw, WA_HEAD_DIM, axis=1)
    zero = jnp.zeros_like(kw)
    variants = (jnp.where(low, kw, zero), jnp.where(low, zero, kw_r),
                jnp.where(low, kw_r, zero), jnp.where(low, zero, kw))
    for hq in range(WA_HEADS):
        kw_ref[hq] = variants[hq].astype(_BF)
    vwt_ref[...] = wkv[:, LANES:].T.astype(_BF)

    cf = seg(SEG_CF, 512)
    ud_ref[...] = cf[:, :256] * jax.nn.sigmoid(cf[:, 256:])


def _in_proj(h, mod, g, w, tabs, *, tm, tiles_per_mod, mod_base, seq_tiles):
    t, d = h.shape
    rope = tabs is not None
    if tiles_per_mod is None:
        mod_map = lambda i: (mod_base, 0, 0)
    else:
        mod_map = lambda i: (i // tiles_per_mod, 0, 0)
    in_specs = [pl.BlockSpec((tm, d), lambda i: (i, 0)),
                pl.BlockSpec((1, N_MOD, d), mod_map),
                _const_spec((1, d)),
                _const_spec((d, N_PACK)),
                _const_spec((1, 256)), _const_spec((256, 1024)),
                _const_spec((1, 128)), _const_spec((128, 768))]
    args = [h, mod, g, w["w_in"], w["g_q"], w["w_uq"], w["g_kv"], w["w_ukv"]]
    if rope:
        in_specs += [pl.BlockSpec((tm, LANES), lambda i: (i % seq_tiles, 0))] * 4
        args += list(tabs)
    row = lambda width: pl.BlockSpec((tm, width), lambda i: (i, 0))
    col = lambda height: pl.BlockSpec((height, tm), lambda i: (0, i))
    out_shape = (
        jax.ShapeDtypeStruct((t, 512), _BF),
        jax.ShapeDtypeStruct((MLA_HEADS, t, LANES), _BF),
        jax.ShapeDtypeStruct((256, t), _BF),
        jax.ShapeDtypeStruct((t, 256), _F32),
        jax.ShapeDtypeStruct((t, 256), _F32),
        jax.ShapeDtypeStruct((t, 256), _BF),
        jax.ShapeDtypeStruct((WA_HEADS, t, LANES), _BF),
        jax.ShapeDtypeStruct((128, t), _BF),
        jax.ShapeDtypeStruct((t, 256), _F32),
    )
    heads = lambda n: pl.BlockSpec((n, tm, LANES), lambda i: (0, i, 0))
    out_specs = (row(512), heads(MLA_HEADS), col(256), row(256), row(256), row(256), heads(WA_HEADS),
                 col(128), row(256))
    return pl.pallas_call(
        functools.partial(_in_proj_kernel, rope=rope),
        out_shape=out_shape,
        grid=(t // tm,),
        in_specs=in_specs,
        out_specs=out_specs,
        scratch_shapes=[pltpu.VMEM((tm, N_PACK), _F32)],
        compiler_params=_params(1),
        name="in_proj",
    )(*args)


def _mla_kernel(q_ref, *refs, has_lat, n_lat_tiles):
    if has_lat:
        kl_ref, vl_ref, kc_ref, vc_ref, o_ref, s_ref, p_ref = refs
    else:
        kc_ref, vc_ref, o_ref, s_ref, p_ref = refs
    tq = q_ref.shape[0]
    n_ctx = kc_ref.shape[1]

    def scores_stage(k_ref, key_slice, tk):
        tile_max = []
        for h in range(MLA_HEADS):
            s = _dot_nt(k_ref[h, key_slice, :],
                        q_ref[:, h * LANES:(h + 1) * LANES])
            s_ref[h, 0:tk, :] = s
            tile_max.append(jnp.max(s, axis=0, keepdims=True))
        return tuple(tile_max)

    def softmax_stage(tk, tile_max, ms):
        new_ms, alphas = [], []
        for h in range(MLA_HEADS):
            m_new = jnp.maximum(ms[h], tile_max[h])
            new_ms.append(m_new)
            alphas.append(jnp.exp2(ms[h] - m_new))
            for r0 in range(0, tk, SM_ROWS):
                p_ref[h, r0:r0 + SM_ROWS, :] = _exp2_bf(s_ref[h, r0:r0 + SM_ROWS, :] - m_new)
        return tuple(new_ms), tuple(alphas)

    def value_stage(vt_ref, key_slice, tk, alphas, accs):
        ones = jnp.ones((ONES_ROWS, tk), _BF)
        new = []
        for h in range(MLA_HEADS):
            v1 = jnp.concatenate([vt_ref[h * MLA_V:(h + 1) * MLA_V, key_slice], ones], axis=0)
            new.append(alphas[h] * accs[h] + _dot(v1, p_ref[h, 0:tk, :]))
        return tuple(new)

    ms = tuple(jnp.full((1, tq), NEG, _F32) for _ in range(MLA_HEADS))
    accs = tuple(jnp.zeros((MLA_V + ONES_ROWS, tq), _F32) for _ in range(MLA_HEADS))
    if has_lat:
        def lat(t):
            return pl.ds(pl.multiple_of(t * TK_MLA, TK_MLA), TK_MLA)

        def body(k, state, ctx_next):
            tile_max, ms, accs = state
            ms, alphas = softmax_stage(TK_MLA, tile_max, ms)
            if ctx_next:
                tile_max = scores_stage(kc_ref, slice(None), n_ctx)
            else:
                tile_max = scores_stage(kl_ref, lat(k + 1), TK_MLA)
            return tile_max, ms, value_stage(vl_ref, lat(k), TK_MLA, alphas, accs)

        state = (scores_stage(kl_ref, lat(0), TK_MLA), ms, accs)
        state = lax.fori_loop(0, n_lat_tiles - 1, functools.partial(body, ctx_next=False), state)
        tile_max, ms, accs = body(n_lat_tiles - 1, state, True)
    else:
        tile_max = scores_stage(kc_ref, slice(None), n_ctx)
    ms, alphas = softmax_stage(n_ctx, tile_max, ms)
    accs = value_stage(vc_ref, slice(None), n_ctx, alphas, accs)
    outs = [acc[:MLA_V] / acc[MLA_V:MLA_V + 1] for acc in accs]
    o_ref[...] = jnp.concatenate(outs, axis=0).T.astype(_BF)


def _mla_attn(q, k_lat, vt_lat, k_ctx, vt_ctx, *, batch, tq):
    t = q.shape[0]
    per_b = t // batch
    nq = per_b // tq
    c = k_ctx.shape[1] // batch
    has_lat = k_lat is not None
    in_specs = [pl.BlockSpec((tq, 512), lambda b, j: (b * nq + j, 0))]
    args = [q]
    n_lat_tiles = 0
    if has_lat:
        s = k_lat.shape[1] // batch
        n_lat_tiles = s // TK_MLA
        in_specs += [pl.BlockSpec((MLA_HEADS, s, LANES), lambda b, j: (0, b, 0)),
                     pl.BlockSpec((256, s), lambda b, j: (0, b))]
        args += [k_lat, vt_lat]
    in_specs += [pl.BlockSpec((MLA_HEADS, c, LANES), lambda b, j: (0, b, 0)),
                 pl.BlockSpec((256, c), lambda b, j: (0, b))]
    args += [k_ctx, vt_ctx]
    return pl.pallas_call(
        functools.partial(_mla_kernel, has_lat=has_lat, n_lat_tiles=n_lat_tiles),
        out_shape=jax.ShapeDtypeStruct((t, 256), _BF),
        grid=(batch, nq),
        in_specs=in_specs,
        out_specs=pl.BlockSpec((tq, 256), lambda b, j: (b * nq + j, 0)),
        scratch_shapes=[pltpu.VMEM((MLA_HEADS, max(TK_MLA, c), tq), _F32),
                        pltpu.VMEM((MLA_HEADS, max(TK_MLA, c), tq), _BF)],
        compiler_params=_params(2),
        name="mla_attn",
    )(*args)


def _wa_kernel(sink_ref, q_ref, *refs, has_lat, seq_len, n_sub):
    if has_lat:
        kl_ref, vl_ref, kc_ref, vc_ref, o_ref, sc_ref, pc_ref, sl_ref, pl_ref, bias_ref = refs
    else:
        kc_ref, vc_ref, o_ref, sc_ref, pc_ref = refs
    tq = q_ref.shape[0] // n_sub
    n_ctx = kc_ref.shape[1]
    win_keys = tq + 2 * WINDOW
    units = [(sub, hq) for sub in range(n_sub) for hq in range(WA_HEADS)]
    local = []
    if has_lat:
        for sub in range(n_sub):
            q0 = (pl.program_id(1) * n_sub + sub) * tq
            start = pl.multiple_of(jnp.clip(q0 - WINDOW, 0, seq_len - win_keys), LANES)
            local.append(pl.ds(start, win_keys))
            kpos = start + lax.broadcasted_iota(jnp.int32, (win_keys, tq), 0)
            qpos = q0 + lax.broadcasted_iota(jnp.int32, (win_keys, tq), 1)
            bias_ref[sub] = jnp.where(jnp.abs(kpos - qpos) <= WINDOW, 0.0, NEG)

    stats = {}
    for sub, hq in units:
        g = hq // (WA_HEADS // WA_KV_HEADS)
        qpair = q_ref[sub * tq:(sub + 1) * tq, g * LANES:(g + 1) * LANES]
        snk = sink_ref[hq] * LOG2E
        s_ctx = _dot_nt(kc_ref[hq], qpair)
        sc_ref[sub, hq] = s_ctx
        m = jnp.maximum(jnp.max(s_ctx, axis=0, keepdims=True), snk)
        if has_lat:
            s_loc = _dot_nt(kl_ref[hq, local[sub], :], qpair) + bias_ref[sub]
            sl_ref[sub, hq] = s_loc
            m = jnp.maximum(m, jnp.max(s_loc, axis=0, keepdims=True))
        stats[sub, hq] = (m, jnp.exp2(snk - m))

    for sub, hq in units:
        m = stats[sub, hq][0]
        for r0 in range(0, n_ctx, SM_ROWS):
            pc_ref[sub, hq, r0:r0 + SM_ROWS, :] = _exp2_bf(sc_ref[sub, hq, r0:r0 + SM_ROWS, :] - m)
        if has_lat:
            for r0 in range(0, win_keys, SM_ROWS):
                pl_ref[sub, hq, r0:r0 + SM_ROWS, :] = _exp2_bf(sl_ref[sub, hq, r0:r0 + SM_ROWS, :] - m)

    for sub in range(n_sub):
        outs = []
        for hq in range(WA_HEADS):
            g = hq // (WA_HEADS // WA_KV_HEADS)
            rows = slice(g * WA_HEAD_DIM, (g + 1) * WA_HEAD_DIM)
            acc = _dot(jnp.concatenate([vc_ref[rows, :], jnp.ones((ONES_ROWS, n_ctx), _BF)], axis=0),
                       pc_ref[sub, hq])
            if has_lat:
                acc = acc + _dot(jnp.concatenate([vl_ref[rows, local[sub]],
                                                  jnp.ones((ONES_ROWS, win_keys), _BF)], axis=0), pl_ref[sub, hq])
            l = acc[WA_HEAD_DIM:WA_HEAD_DIM + 1] + stats[sub, hq][1]
            outs.append(acc[:WA_HEAD_DIM] / l)
        o_ref[sub * tq:(sub + 1) * tq, :] = jnp.concatenate(outs, axis=0).T.astype(_BF)


def _wa_attn(sink, q, k_lat, vt_lat, k_ctx, vt_ctx, *, batch, tq, n_sub):
    t = q.shape[0]
    per_b = t // batch
    step_q = tq * n_sub
    nq = per_b // step_q
    c = k_ctx.shape[1] // batch
    has_lat = k_lat is not None
    in_specs = [pl.BlockSpec(memory_space=pltpu.SMEM),
                pl.BlockSpec((step_q, 256), lambda b, j: (b * nq + j, 0))]
    args = [sink, q]
    seq_len = 0
    if has_lat:
        seq_len = k_lat.shape[1] // batch
        in_specs += [pl.BlockSpec((WA_HEADS, seq_len, LANES), lambda b, j: (0, b, 0)),
                     pl.BlockSpec((128, seq_len), lambda b, j: (0, b))]
        args += [k_lat, vt_lat]
    in_specs += [pl.BlockSpec((WA_HEADS, c, LANES), lambda b, j: (0, b, 0)),
                 pl.BlockSpec((128, c), lambda b, j: (0, b))]
    args += [k_ctx, vt_ctx]
    win_keys = tq + 2 * WINDOW
    scratch = [pltpu.VMEM((n_sub, WA_HEADS, c, tq), _F32), pltpu.VMEM((n_sub, WA_HEADS, c, tq), _BF)]
    if has_lat:
        scratch += [pltpu.VMEM((n_sub, WA_HEADS, win_keys, tq), _F32),
                    pltpu.VMEM((n_sub, WA_HEADS, win_keys, tq), _BF),
                    pltpu.VMEM((n_sub, win_keys, tq), _F32)]
    return pl.pallas_call(
        functools.partial(_wa_kernel, has_lat=has_lat, seq_len=seq_len, n_sub=n_sub),
        out_shape=jax.ShapeDtypeStruct((t, 256), _BF),
        grid=(batch, nq),
        in_specs=in_specs,
        out_specs=pl.BlockSpec((step_q, 256), lambda b, j: (b * nq + j, 0)),
        scratch_shapes=scratch,
        compiler_params=_params(2),
        name="wa_attn",
    )(*args)


HALO_SC = 8
HALO_CF = 16
CONV_ROWS = 64
SC_PHASES = ((HALO_SC - SC_K // 2) % SUBLANES, (HALO_SC + SC_K // 2) % SUBLANES)
CF_PHASES = tuple(range(1, SUBLANES))


def _mix_out_kernel(h_ref, mod_ref, oa_ref, oc_ref, bg_ref, cx_ref, ud_ref, *rest, halo, seq_tiles):
    if halo:
        cxp_ref, cxn_ref, udp_ref, udn_ref = rest[:4]
        rest = rest[4:]
    (wsc_ref, wcf_ref, bcf_ref, gln_ref, bln_ref, wout_ref, o_ref,
     xsc_ref, xcf_ref, ssc_ref, scf_ref, ob_ref, od_ref) = rest
    tm = h_ref.shape[0]
    width = cx_ref.shape[1]

    if halo:
        i = pl.program_id(0) % seq_tiles
        has_prev = (i != 0).astype(_F32)
        has_next = (i != seq_tiles - 1).astype(_F32)
        xsc_ref[0:HALO_SC, :] = cxp_ref[...] * has_prev
        xsc_ref[HALO_SC + tm:, :] = cxn_ref[...] * has_next
        xcf_ref[0:HALO_CF, :] = udp_ref[...] * has_prev
        xcf_ref[HALO_CF + tm:, :] = udn_ref[...] * has_next
    else:
        xsc_ref[0:HALO_SC, :] = jnp.zeros((HALO_SC, width), _F32)
        xsc_ref[HALO_SC + tm:, :] = jnp.zeros((HALO_SC, width), _F32)
        xcf_ref[0:HALO_CF, :] = jnp.zeros((HALO_CF, width), _F32)
        xcf_ref[HALO_CF + tm:, :] = jnp.zeros((HALO_CF, width), _F32)
    xsc_ref[HALO_SC:HALO_SC + tm, :] = cx_ref[...]
    xcf_ref[HALO_CF:HALO_CF + tm, :] = ud_ref[...]

    for r in range(1, SUBLANES):
        scf_ref[r - 1] = xcf_ref[r:r + scf_ref.shape[1], :]
    for n, r in enumerate(SC_PHASES):
        ssc_ref[n] = xsc_ref[r:r + ssc_ref.shape[1], :]

    def tap(x_ref, copies_ref, phases, row):
        r = row % SUBLANES
        base = row - r
        if r == 0:
            return x_ref[base:base + CONV_ROWS, :]
        return copies_ref[phases.index(r), base:base + CONV_ROWS, :]

    for r0 in range(0, tm, CONV_ROWS):
        acc = jnp.zeros((CONV_ROWS, width), _F32)
        for k in range(SC_K):
            row = HALO_SC + r0 + k - SC_K // 2
            acc = acc + tap(xsc_ref, ssc_ref, SC_PHASES, row) * wsc_ref[k:k + 1, :]
        ob_ref[r0:r0 + CONV_ROWS, :] = (bg_ref[r0:r0 + CONV_ROWS, :] * acc).astype(_BF)

        acc = jnp.zeros((CONV_ROWS, width), _F32)
        for k in range(CF_K):
            row = HALO_CF + r0 + k - CF_K // 2
            acc = acc + tap(xcf_ref, scf_ref, CF_PHASES, row) * wcf_ref[k:k + 1, :]
        u = acc + bcf_ref[...]
        mu = jnp.mean(u, axis=-1, keepdims=True)
        uc = u - mu
        var = jnp.mean(uc * uc, axis=-1, keepdims=True)
        y = uc * lax.rsqrt(var + EPS) * gln_ref[...] + bln_ref[...]
        od_ref[r0:r0 + CONV_ROWS, :] = (y * jax.nn.sigmoid(y)).astype(_BF)

    mixed = jnp.concatenate([oa_ref[...], ob_ref[...], oc_ref[...], od_ref[...]], axis=1)
    o_ref[...] = h_ref[...] + mod_ref[0, 5:6, :] * _dot(mixed, wout_ref[0])


def _mix_out(h, mod, oa, oc, bg, cx, ud, w, w_out, *, layer, tm, tiles_per_mod, mod_base, seq_tiles):
    t, d = h.shape
    halo = seq_tiles > 1
    if tiles_per_mod is None:
        mod_map = lambda i: (mod_base, 0, 0)
    else:
        mod_map = lambda i: (i // tiles_per_mod, 0, 0)
    row = lambda width: pl.BlockSpec((tm, width), lambda i: (i, 0))
    in_specs = [row(d), pl.BlockSpec((1, N_MOD, d), mod_map), row(256), row(256), row(256), row(256), row(256)]
    args = [h, mod, oa, oc, bg, cx, ud]
    if halo:
        nsc = tm // HALO_SC
        ncf = tm // HALO_CF
        last_sc = t // HALO_SC - 1
        last_cf = t // HALO_CF - 1
        in_specs += [
            pl.BlockSpec((HALO_SC, 256), lambda i: (jnp.maximum(i * nsc - 1, 0), 0)),
            pl.BlockSpec((HALO_SC, 256), lambda i: (jnp.minimum((i + 1) * nsc, last_sc), 0)),
            pl.BlockSpec((HALO_CF, 256), lambda i: (jnp.maximum(i * ncf - 1, 0), 0)),
            pl.BlockSpec((HALO_CF, 256), lambda i: (jnp.minimum((i + 1) * ncf, last_cf), 0)),
        ]
        args += [cx, cx, ud, ud]
    in_specs += [_const_spec((SC_K, 256)), _const_spec((CF_K, 256)), _const_spec((1, 256)),
                 _const_spec((1, 256)), _const_spec((1, 256)), _layer_spec((d, d), layer)]
    args += [w["w_sc"], w["w_cf"], w["b_cf"], w["g_ln"], w["b_ln"], w_out]
    return pl.pallas_call(
        functools.partial(_mix_out_kernel, halo=halo, seq_tiles=seq_tiles),
        out_shape=jax.ShapeDtypeStruct((t, d), _F32),
        grid=(t // tm,),
        in_specs=in_specs,
        out_specs=row(d),
        scratch_shapes=[pltpu.VMEM((tm + 2 * HALO_SC, 256), _F32),
                        pltpu.VMEM((tm + 2 * HALO_CF, 256), _F32),
                        pltpu.VMEM((len(SC_PHASES), tm + HALO_SC, 256), _F32),
                        pltpu.VMEM((len(CF_PHASES), tm + 2 * HALO_CF - SUBLANES, 256), _F32),
                        pltpu.VMEM((tm, 256), _BF),
                        pltpu.VMEM((tm, 256), _BF)],
        compiler_params=_params(1),
        name="mix_out",
    )(*args)


def _swap_halves(w, d_rot):
    m = d_rot // 4
    lead = w.shape[:-1]
    return jnp.flip(w.reshape(lead + (-1, 2, m)), axis=-2).reshape(w.shape)


def _rope_table(seq, d_rot, lanes_before):
    m = d_rot // 4
    d_ax = d_rot // 2
    rows = seq // GRID_W
    inv = ROPE_BASE ** (-jnp.arange(0, d_ax, 2, dtype=_F32) / d_ax)
    ar = jnp.arange(rows, dtype=_F32)[:, None] * inv[None, :]
    ac = jnp.arange(GRID_W, dtype=_F32)[:, None] * inv[None, :]

    def per_row(v):
        return jnp.broadcast_to(v[:, None, :], (rows, GRID_W, m)).reshape(seq, m)

    def per_col(v):
        return jnp.broadcast_to(v[None, :, :], (rows, GRID_W, m)).reshape(seq, m)

    cr, sr, cc, sc = per_row(jnp.cos(ar)), per_row(jnp.sin(ar)), per_col(jnp.cos(ac)), per_col(jnp.sin(ac))
    cos = jnp.concatenate([cr, cr, cc, cc], axis=1)
    sin = jnp.concatenate([-sr, sr, -sc, sc], axis=1)
    reps = (LANES - lanes_before) // d_rot if lanes_before == 0 else 1
    pad = LANES - lanes_before - reps * d_rot
    c_tab = jnp.concatenate([jnp.ones((seq, lanes_before), _F32)] + [cos] * reps + [jnp.ones((seq, pad), _F32)], axis=1)
    s_tab = jnp.concatenate([jnp.zeros((seq, lanes_before), _F32)] + [sin] * reps + [jnp.zeros((seq, pad), _F32)], axis=1)
    return c_tab, s_tab


def _pack_layer(p, l):
    offs = np.concatenate([[0], np.cumsum(IN_SIZES)])
    o_cq, o_ckv, o_kr, o_sc, o_wq, o_wkv, o_cf, o_end = (int(v) for v in offs)
    scale_a = float((MLA_NOPE + MLA_ROPE) ** -0.5 * LOG2E)
    scale_w = float(WA_HEAD_DIM ** -0.5 * LOG2E)
    w = p["w_in"][l]
    d = w.shape[0]
    zeros = lambda n: jnp.zeros((d, n), _F32)

    kr = w[:, o_kr:o_sc]
    wq = w[:, o_wq:o_wkv] * scale_w
    nk = WA_KV_HEADS * WA_HEAD_DIM
    wk = w[:, o_wkv:o_wkv + nk]
    pad_r = LANES - MLA_NOPE - MLA_ROPE
    w_in = jnp.concatenate([
        w[:, o_cq:o_ckv], zeros(256 - MLA_Q_RANK),
        zeros(MLA_NOPE), kr, zeros(pad_r), w[:, o_ckv:o_kr],
        w[:, o_sc:o_wq],
        wq,
        w[:, o_cf:o_end],
        wk, w[:, o_wkv + nk:o_cf],
    ], axis=1).astype(_BF)

    wu = p["w_mla_uq"][l].reshape(MLA_Q_RANK, MLA_HEADS, MLA_NOPE + MLA_ROPE) * scale_a
    nope, rope = wu[..., :MLA_NOPE], wu[..., MLA_NOPE:]
    zq = lambda n: jnp.zeros((MLA_Q_RANK, MLA_HEADS, n), _F32)
    w_uq = jnp.concatenate([
        jnp.concatenate([nope, rope, zq(pad_r)], axis=-1).reshape(MLA_Q_RANK, MLA_HEADS * LANES),
        jnp.concatenate([zq(MLA_NOPE), _swap_halves(rope, MLA_ROPE), zq(pad_r)],
                        axis=-1).reshape(MLA_Q_RANK, MLA_HEADS * LANES),
    ], axis=1)
    w_uq = jnp.pad(w_uq, ((0, 256 - MLA_Q_RANK), (0, 0))).astype(_BF)
    g_q = jnp.pad(p["g_mla_q"][l], (0, 256 - MLA_Q_RANK)).reshape(1, 256)

    wkv = p["w_mla_ukv"][l].reshape(MLA_KV_RANK, MLA_HEADS, MLA_NOPE + MLA_V)
    k_nope = jnp.concatenate(
        [wkv[..., :MLA_NOPE], jnp.zeros((MLA_KV_RANK, MLA_HEADS, LANES - MLA_NOPE), _F32)], axis=-1)
    w_ukv = jnp.concatenate([k_nope.reshape(MLA_KV_RANK, MLA_HEADS * LANES),
                             wkv[..., MLA_NOPE:].reshape(MLA_KV_RANK, MLA_HEADS * MLA_V)], axis=1).astype(_BF)

    return dict(
        w_in=w_in, w_uq=w_uq, g_q=g_q, w_ukv=w_ukv, g_kv=p["g_mla_kv"][l].reshape(1, MLA_KV_RANK),
        w_sc=p["w_sc_conv"][l], w_cf=p["w_cf_conv"][l], b_cf=p["b_cf_conv"][l].reshape(1, -1),
        g_ln=p["g_cf_ln"][l].reshape(1, -1), b_ln=p["b_cf_ln"][l].reshape(1, -1),
        g_ffn1=p["g_ffn1"][l].reshape(1, -1), g_mix=p["g_mix"][l].reshape(1, -1),
        g_ffn2=p["g_ffn2"][l].reshape(1, -1),
        sink=p["wa_sink"][l],
    )


def kernel(x, c, ctx, c_ctx, w_mod, b_mod, g_ffn1, w1_gate, w1_up, w1_down, g_mix, w_in, g_mla_q, w_mla_uq, g_mla_kv, w_mla_ukv, w_sc_conv, wa_sink, w_cf_conv, b_cf_conv, g_cf_ln, b_cf_ln, w_out, g_ffn2, w2_gate, w2_up, w2_down, g_final):
    b, s, d = x.shape
    n_ctx = ctx.shape[1]
    depth = w_mod.shape[0]
    p = dict(w_in=w_in, g_mla_q=g_mla_q, w_mla_uq=w_mla_uq, g_mla_kv=g_mla_kv, w_mla_ukv=w_mla_ukv,
             w_sc_conv=w_sc_conv, wa_sink=wa_sink, w_cf_conv=w_cf_conv, b_cf_conv=b_cf_conv,
             g_cf_ln=g_cf_ln, b_cf_ln=b_cf_ln, w_out=w_out, g_ffn1=g_ffn1, g_mix=g_mix, g_ffn2=g_ffn2,
             w1_gate=w1_gate, w1_up=w1_up, w1_down=w1_down, w2_gate=w2_gate, w2_up=w2_up, w2_down=w2_down)

    tm = TM_LAT
    seq_tiles = s // tm
    tabs = _rope_table(s, MLA_ROPE, MLA_NOPE) + _rope_table(s, WA_HEAD_DIM, 0)

    mod_rows = 8
    cs = jnp.concatenate([c, c_ctx[None, :], jnp.zeros((mod_rows - b - 1, d), _F32)], axis=0)
    mod_all = _modulation(cs, w_mod, b_mod).reshape(depth, mod_rows, N_MOD, d)

    lat = dict(tm=tm, tiles_per_mod=seq_tiles, mod_base=0)
    cxt = dict(tm=n_ctx, tiles_per_mod=None, mod_base=b)

    w1 = (_cast_bf16(w1_gate), _cast_bf16(w1_up), _cast_bf16(w1_down))
    w2 = (_cast_bf16(w2_gate), _cast_bf16(w2_up), _cast_bf16(w2_down))
    w_out_b = _cast_bf16(w_out)

    h_lat = x.reshape(b * s, d)
    h_ctx = ctx.reshape(b * n_ctx, d)
    for l in range(depth):
        last = l == depth - 1
        w = _pack_layer(p, l)
        mod = mod_all[l]

        h_lat = _ffn(h_lat, mod, w["g_ffn1"], *w1, layer=l, which=0, n_sub=FFN_SUBTILES, **lat)
        h_ctx = _ffn(h_ctx, mod, w["g_ffn1"], *w1, layer=l, which=0, **cxt)

        qa, ka, vat, bg, cx, qw, kw, vwt, ud = _in_proj(h_lat, mod, w["g_mix"], w, tabs, seq_tiles=seq_tiles, **lat)
        qa_c, ka_c, vat_c, bg_c, cx_c, qw_c, kw_c, vwt_c, ud_c = _in_proj(
            h_ctx, mod, w["g_mix"], w, None, seq_tiles=1, **cxt)

        oa = _mla_attn(qa, ka, vat, ka_c, vat_c, batch=b, tq=TQ_MLA)
        oc = _wa_attn(w["sink"], qw, kw, vwt, kw_c, vwt_c, batch=b, tq=TQ_WA, n_sub=WA_SUBTILES)
        h_lat = _mix_out(h_lat, mod, oa, oc, bg, cx, ud, w, w_out_b, layer=l, seq_tiles=seq_tiles, **lat)

        if not last:
            oa_c = _mla_attn(qa_c, None, None, ka_c, vat_c, batch=b, tq=n_ctx)
            oc_c = _wa_attn(w["sink"], qw_c, None, None, kw_c, vwt_c, batch=b, tq=n_ctx, n_sub=1)
            h_ctx = _mix_out(h_ctx, mod, oa_c, oc_c, bg_c, cx_c, ud_c, w, w_out_b, layer=l, seq_tiles=1, **cxt)
            h_ctx = _ffn(h_ctx, mod, w["g_ffn2"], *w2, layer=l, which=2, **cxt)

        h_lat = _ffn(h_lat, mod, w["g_ffn2"], *w2, layer=l, which=2,
                     g_final=g_final.reshape(1, d) if last else None, n_sub=FFN_SUBTILES, **lat)

    return h_lat.reshape(b, s, d)
```

```python
import functools

import numpy as np
import jax
import jax.numpy as jnp
from jax import lax
from jax.experimental import pallas as pl
from jax.experimental.pallas import tpu as pltpu

GRID_W = 64
ROPE_BASE = 10000.0
EPS = 1e-6
NEG = -1e30
N_MOD = 9
LOG2E = 1.4426950408889634

MLA_HEADS = 4
MLA_NOPE = 64
MLA_ROPE = 32
MLA_V = 64
MLA_Q_RANK = 192
MLA_KV_RANK = 128
SC_WIDTH = 256
SC_K = 3
WA_HEADS = 4
WA_KV_HEADS = 2
WA_HEAD_DIM = 64
WINDOW = 128
CF_WIDTH = 256
CF_K = 31
IN_SIZES = (MLA_Q_RANK, MLA_KV_RANK, MLA_ROPE, 3 * SC_WIDTH,
            WA_HEADS * WA_HEAD_DIM, 2 * WA_KV_HEADS * WA_HEAD_DIM, 2 * CF_WIDTH)

LANES = 128
SUBLANES = 8
MXU_N = 256
SM_ROWS = 32
ONES_ROWS = 16

_BF = jnp.bfloat16
_F32 = jnp.float32

SEG_CQ = 0
SEG_KRC = 256
SEG_SC = 512
SEG_WQ = 1280
SEG_CF = 1536
SEG_WKV = 2048
N_PACK = 2304

TM_LAT = 512
TQ_MLA = 512
TK_MLA = 1024
TQ_WA = 256
WA_SUBTILES = 8
FF_CHUNK = 256
CAST_BLOCK_BYTES = 12 * 1024 * 1024
FFN_SUBTILES = 2
VMEM_LIMIT = 52 * 1024 * 1024


def _params(n_axes, flags=None):
    return pltpu.CompilerParams(dimension_semantics=("arbitrary",) * n_axes,
                                vmem_limit_bytes=VMEM_LIMIT, flags=flags)


def _const_spec(shape):
    nd = len(shape)
    return pl.BlockSpec(shape, lambda *_: (0,) * nd, pipeline_mode=pl.Buffered(1))


def _layer_spec(shape, layer):
    return pl.BlockSpec((1,) + tuple(shape), lambda *_: (layer, 0, 0), pipeline_mode=pl.Buffered(1))


def _cast_kernel(x_ref, o_ref):
    o_ref[...] = x_ref[...].astype(o_ref.dtype)


def _cast_bf16(w):
    depth, rows, cols = w.shape
    tr = rows
    while tr * cols * 4 > CAST_BLOCK_BYTES:
        tr //= 2
    return pl.pallas_call(
        _cast_kernel,
        out_shape=jax.ShapeDtypeStruct(w.shape, _BF),
        grid=(depth, rows // tr),
        in_specs=[pl.BlockSpec((1, tr, cols), lambda l, i: (l, i, 0))],
        out_specs=pl.BlockSpec((1, tr, cols), lambda l, i: (l, i, 0)),
        compiler_params=_params(2),
        name="cast_bf16",
    )(w)


def _dot(a, b):
    return jnp.dot(a, b, preferred_element_type=_F32)


def _dot_nt(a, b):
    return lax.dot_general(a, b, (((1,), (1,)), ((), ())), preferred_element_type=_F32)


def _exp2_bf(d):
    return jnp.exp2(d).astype(_BF)


def _swap_lanes(x, m):
    lane = lax.broadcasted_iota(jnp.int32, x.shape, 1)
    low = (lane % (2 * m)) < m
    return jnp.where(low, pltpu.roll(x, LANES - m, axis=1), pltpu.roll(x, m, axis=1))


def _norm_mod(x, g, shift, scale):
    y = x * lax.rsqrt(jnp.mean(x * x, axis=-1, keepdims=True) + EPS) * g
    return y * (1.0 + scale) + shift


def _mod_kernel(c_ref, w_ref, b_ref, o_ref):
    c = c_ref[...]
    a = c * jax.nn.sigmoid(c)
    rows = a.shape[0]
    a_hi = a.astype(_BF).astype(_F32)
    lhs = jnp.concatenate([a_hi, a - a_hi], axis=0).astype(_BF)
    w = w_ref[0]
    w_hi = w.astype(_BF)
    w_lo = (w - w_hi.astype(_F32)).astype(_BF)
    r = _dot(lhs, w_hi) + _dot(lhs, w_lo)
    o_ref[0] = r[:rows] + r[rows:] + b_ref[0]


def _modulation(cs, w_mod, b_mod):
    depth, d, n = w_mod.shape
    tn = 1024
    rows = cs.shape[0]
    return pl.pallas_call(
        _mod_kernel,
        out_shape=jax.ShapeDtypeStruct((depth, rows, n), _F32),
        grid=(depth, n // tn),
        in_specs=[pl.BlockSpec((rows, d), lambda l, j: (0, 0)),
                  pl.BlockSpec((1, d, tn), lambda l, j: (l, 0, j)),
                  pl.BlockSpec((1, 1, tn), lambda l, j: (l, 0, j))],
        out_specs=pl.BlockSpec((1, rows, tn), lambda l, j: (l, 0, j)),
        compiler_params=_params(2),
        name="modulation",
    )(cs, w_mod, b_mod.reshape(depth, 1, n))


def _ffn_kernel(h_ref, mod_ref, g_ref, wg_ref, wu_ref, wd_ref, *rest, which, final, n_sub):
    if final:
        gf_ref, o_ref, a_ref = rest
    else:
        o_ref, a_ref = rest
    shift = mod_ref[0, 3 * which:3 * which + 1, :]
    scale = mod_ref[0, 3 * which + 1:3 * which + 2, :]
    gate = mod_ref[0, 3 * which + 2:3 * which + 3, :]
    d_ff = wg_ref.shape[2]
    rows_per = h_ref.shape[0] // n_sub
    for sub in range(n_sub):
        rows = slice(sub * rows_per, (sub + 1) * rows_per)
        x = h_ref[rows, :]
        xb = _norm_mod(x, g_ref[...], shift, scale).astype(_BF)
        for j in range(d_ff // FF_CHUNK):
            sl = slice(j * FF_CHUNK, (j + 1) * FF_CHUNK)
            gt = _dot(xb, wg_ref[0, :, sl])
            up = _dot(xb, wu_ref[0, :, sl])
            a_ref[rows, sl] = (gt * jax.nn.sigmoid(gt) * up).astype(_BF)
        y = x + 0.5 * gate * _dot(a_ref[rows, :], wd_ref[0])
        if final:
            y = y * lax.rsqrt(jnp.mean(y * y, axis=-1, keepdims=True) + EPS) * gf_ref[...]
        o_ref[rows, :] = y


def _ffn(h, mod, g, wg, wu, wd, *, layer, which, tm, tiles_per_mod, mod_base, g_final=None, n_sub=1):
    t, d = h.shape
    d_ff = wg.shape[2]
    final = g_final is not None
    tm = tm * n_sub
    if tiles_per_mod is None:
        mod_map = lambda i: (mod_base, 0, 0)
    else:
        mod_map = lambda i: (i // (tiles_per_mod // n_sub), 0, 0)
    in_specs = [pl.BlockSpec((tm, d), lambda i: (i, 0)),
                pl.BlockSpec((1, N_MOD, d), mod_map),
                _const_spec((1, d)),
                _layer_spec((d, d_ff), layer), _layer_spec((d, d_ff), layer), _layer_spec((d_ff, d), layer)]
    args = [h, mod, g, wg, wu, wd]
    if final:
        in_specs.append(_const_spec((1, d)))
        args.append(g_final)
    return pl.pallas_call(
        functools.partial(_ffn_kernel, which=which, final=final, n_sub=n_sub),
        out_shape=jax.ShapeDtypeStruct((t, d), _F32),
        grid=(t // tm,),
        in_specs=in_specs,
        out_specs=pl.BlockSpec((tm, d), lambda i: (i, 0)),
        scratch_shapes=[pltpu.VMEM((tm, d_ff), _BF)],
        compiler_params=_params(1),
        name="ffn",
    )(*args)


def _in_proj_kernel(h_ref, mod_ref, g_ref, win_ref, gq_ref, wuq_ref, gkv_ref, wukv_ref, *rest, rope):
    if rope:
        tab_refs = rest[:8]
        rest = rest[8:]
    qa_ref, ka_ref, vat_ref, bg_ref, cx_ref, qw_ref, kw_ref, vwt_ref, ud_ref, z_ref = rest
    if rope:
        def expand(row_ref, col_ref, row_lanes):
            col = col_ref[...]
            return jnp.concatenate([jnp.where(row_lanes, row_ref[r:r + 1, :], col)
                                    for r in range(h_ref.shape[0] // GRID_W)], axis=0)

        lane = lax.broadcasted_iota(jnp.int32, (GRID_W, LANES), 1)
        rows_a = jnp.abs(2 * (lane - MLA_NOPE) - (MLA_ROPE // 2 - 1)) < MLA_ROPE // 2
        rows_w = (lane % WA_HEAD_DIM) < WA_HEAD_DIM // 2
        ca, sa = expand(tab_refs[0], tab_refs[1], rows_a), expand(tab_refs[2], tab_refs[3], rows_a)
        cw, sw = expand(tab_refs[4], tab_refs[5], rows_w), expand(tab_refs[6], tab_refs[7], rows_w)

    x = h_ref[...]
    xb = _norm_mod(x, g_ref[...], mod_ref[0, 3:4, :], mod_ref[0, 4:5, :]).astype(_BF)

    z_ref[...] = _dot(xb, win_ref[...])

    def seg(lo, width):
        return z_ref[:, lo:lo + width]

    cq = seg(SEG_CQ, 256)
    cqn = cq * lax.rsqrt(jnp.sum(cq * cq, axis=-1, keepdims=True) * (1.0 / MLA_Q_RANK) + EPS) * gq_ref[...]
    qq = _dot(cqn.astype(_BF), wuq_ref[...])
    q = qq[:, :MLA_HEADS * LANES]
    if rope:
        q = (q * jnp.concatenate([ca] * MLA_HEADS, axis=1)
             + qq[:, MLA_HEADS * LANES:] * jnp.concatenate([sa] * MLA_HEADS, axis=1))
    qa_ref[...] = q.astype(_BF)

    krc = seg(SEG_KRC, 256)
    ckv = krc[:, LANES:]
    ckvn = ckv * lax.rsqrt(jnp.mean(ckv * ckv, axis=-1, keepdims=True) + EPS) * gkv_ref[...]
    kv = _dot(ckvn.astype(_BF), wukv_ref[...])
    kr = krc[:, :LANES]
    if rope:
        kr = kr * ca + _swap_lanes(kr, MLA_ROPE // 4) * sa
    for h in range(MLA_HEADS):
        ka_ref[h] = (kv[:, h * LANES:(h + 1) * LANES] + kr).astype(_BF)
    vat_ref[...] = kv[:, MLA_HEADS * LANES:].T.astype(_BF)

    sc = seg(SEG_SC, 768)
    bg_ref[...] = sc[:, :256]
    cx_ref[...] = sc[:, 256:512] * sc[:, 512:]

    qw = seg(SEG_WQ, 256)
    wkv = seg(SEG_WKV, 256)
    kw = wkv[:, :LANES]
    if rope:
        m_w = WA_HEAD_DIM // 4
        qw = jnp.concatenate([qw[:, :LANES] * cw + _swap_lanes(qw[:, :LANES], m_w) * sw,
                              qw[:, LANES:] * cw + _swap_lanes(qw[:, LANES:], m_w) * sw], axis=1)
        kw = kw * cw + _swap_lanes(kw, m_w) * sw
    qw_ref[...] = qw.astype(_BF)
    lane = lax.broadcasted_iota(jnp.int32, kw.shape, 1)
    low = lane < WA_HEAD_DIM
    kw_r = pltpu.roll(kw, WA_HEAD_DIM, axis=1)
    zero = jnp.zeros_like(kw)
    variants = (jnp.where(low, kw, zero), jnp.where(low, zero, kw_r),
                jnp.where(low, kw_r, zero), jnp.where(low, zero, kw))
    for hq in range(WA_HEADS):
        kw_ref[hq] = variants[hq].astype(_BF)
    vwt_ref[...] = wkv[:, LANES:].T.astype(_BF)

    cf = seg(SEG_CF, 512)
    ud_ref[...] = cf[:, :256] * jax.nn.sigmoid(cf[:, 256:])


def _in_proj(h, mod, g, w, tabs, *, tm, tiles_per_mod, mod_base, seq_tiles):
    t, d = h.shape
    rope = tabs is not None
    if tiles_per_mod is None:
        mod_map = lambda i: (mod_base, 0, 0)
    else:
        mod_map = lambda i: (i // tiles_per_mod, 0, 0)
    in_specs = [pl.BlockSpec((tm, d), lambda i: (i, 0)),
                pl.BlockSpec((1, N_MOD, d), mod_map),
                _const_spec((1, d)),
                _const_spec((d, N_PACK)),
                _const_spec((1, 256)), _const_spec((256, 1024)),
                _const_spec((1, 128)), _const_spec((128, 768))]
    args = [h, mod, g, w["w_in"], w["g_q"], w["w_uq"], w["g_kv"], w["w_ukv"]]
    if rope:
        row_tab = pl.BlockSpec((tm // GRID_W, LANES), lambda i: (i % seq_tiles, 0))
        in_specs += [row_tab, _const_spec((GRID_W, LANES))] * 4
        args += list(tabs)
    row = lambda width: pl.BlockSpec((tm, width), lambda i: (i, 0))
    col = lambda height: pl.BlockSpec((height, tm), lambda i: (0, i))
    out_shape = (
        jax.ShapeDtypeStruct((t, 512), _BF),
        jax.ShapeDtypeStruct((MLA_HEADS, t, LANES), _BF),
        jax.ShapeDtypeStruct((256, t), _BF),
        jax.ShapeDtypeStruct((t, 256), _F32),
        jax.ShapeDtypeStruct((t, 256), _F32),
        jax.ShapeDtypeStruct((t, 256), _BF),
        jax.ShapeDtypeStruct((WA_HEADS, t, LANES), _BF),
        jax.ShapeDtypeStruct((128, t), _BF),
        jax.ShapeDtypeStruct((t, 256), _F32),
    )
    heads = lambda n: pl.BlockSpec((n, tm, LANES), lambda i: (0, i, 0))
    out_specs = (row(512), heads(MLA_HEADS), col(256), row(256), row(256), row(256), heads(WA_HEADS),
                 col(128), row(256))
    return pl.pallas_call(
        functools.partial(_in_proj_kernel, rope=rope),
        out_shape=out_shape,
        grid=(t // tm,),
        in_specs=in_specs,
        out_specs=out_specs,
        scratch_shapes=[pltpu.VMEM((tm, N_PACK), _F32)],
        compiler_params=_params(1),
        name="in_proj",
    )(*args)


def _mla_kernel(q_ref, *refs, has_lat, n_lat_tiles):
    if has_lat:
        kl_ref, vl_ref, kc_ref, vc_ref, o_ref, s_ref, p_ref = refs
    else:
        kc_ref, vc_ref, o_ref, s_ref, p_ref = refs
    tq = q_ref.shape[0]
    n_ctx = kc_ref.shape[1]

    def scores_stage(k_ref, key_slice, tk):
        tile_max = []
        for h in range(MLA_HEADS):
            s = _dot_nt(k_ref[h, key_slice, :],
                        q_ref[:, h * LANES:(h + 1) * LANES])
            s_ref[h, 0:tk, :] = s
            tile_max.append(jnp.max(s, axis=0, keepdims=True))
        return tuple(tile_max)

    def softmax_stage(tk, tile_max, ms):
        new_ms, alphas = [], []
        for h in range(MLA_HEADS):
            m_new = jnp.maximum(ms[h], tile_max[h])
            new_ms.append(m_new)
            alphas.append(jnp.exp2(ms[h] - m_new))
            for r0 in range(0, tk, SM_ROWS):
                p_ref[h, r0:r0 + SM_ROWS, :] = _exp2_bf(s_ref[h, r0:r0 + SM_ROWS, :] - m_new)
        return tuple(new_ms), tuple(alphas)

    def value_stage(vt_ref, key_slice, tk, alphas, accs):
        ones = jnp.ones((ONES_ROWS, tk), _BF)
        new = []
        for h in range(MLA_HEADS):
            v1 = jnp.concatenate([vt_ref[h * MLA_V:(h + 1) * MLA_V, key_slice], ones], axis=0)
            new.append(alphas[h] * accs[h] + _dot(v1, p_ref[h, 0:tk, :]))
        return tuple(new)

    ms = tuple(jnp.full((1, tq), NEG, _F32) for _ in range(MLA_HEADS))
    accs = tuple(jnp.zeros((MLA_V + ONES_ROWS, tq), _F32) for _ in range(MLA_HEADS))
    if has_lat:
        def lat(t):
            return pl.ds(pl.multiple_of(t * TK_MLA, TK_MLA), TK_MLA)

        def body(k, state, ctx_next):
            tile_max, ms, accs = state
            ms, alphas = softmax_stage(TK_MLA, tile_max, ms)
            if ctx_next:
                tile_max = scores_stage(kc_ref, slice(None), n_ctx)
            else:
                tile_max = scores_stage(kl_ref, lat(k + 1), TK_MLA)
            return tile_max, ms, value_stage(vl_ref, lat(k), TK_MLA, alphas, accs)

        state = (scores_stage(kl_ref, lat(0), TK_MLA), ms, accs)
        state = lax.fori_loop(0, n_lat_tiles - 1, functools.partial(body, ctx_next=False), state)
        tile_max, ms, accs = body(n_lat_tiles - 1, state, True)
    else:
        tile_max = scores_stage(kc_ref, slice(None), n_ctx)
    ms, alphas = softmax_stage(n_ctx, tile_max, ms)
    accs = value_stage(vc_ref, slice(None), n_ctx, alphas, accs)
    outs = [acc[:MLA_V] / acc[MLA_V:MLA_V + 1] for acc in accs]
    o_ref[...] = jnp.concatenate(outs, axis=0).T.astype(_BF)


def _mla_attn(q, k_lat, vt_lat, k_ctx, vt_ctx, *, batch, tq):
    t = q.shape[0]
    per_b = t // batch
    nq = per_b // tq
    c = k_ctx.shape[1] // batch
    has_lat = k_lat is not None
    in_specs = [pl.BlockSpec((tq, 512), lambda b, j: (b * nq + j, 0))]
    args = [q]
    n_lat_tiles = 0
    if has_lat:
        s = k_lat.shape[1] // batch
        n_lat_tiles = s // TK_MLA
        in_specs += [pl.BlockSpec((MLA_HEADS, s, LANES), lambda b, j: (0, b, 0)),
                     pl.BlockSpec((256, s), lambda b, j: (0, b))]
        args += [k_lat, vt_lat]
    in_specs += [pl.BlockSpec((MLA_HEADS, c, LANES), lambda b, j: (0, b, 0)),
                 pl.BlockSpec((256, c), lambda b, j: (0, b))]
    args += [k_ctx, vt_ctx]
    return pl.pallas_call(
        functools.partial(_mla_kernel, has_lat=has_lat, n_lat_tiles=n_lat_tiles),
        out_shape=jax.ShapeDtypeStruct((t, 256), _BF),
        grid=(batch, nq),
        in_specs=in_specs,
        out_specs=pl.BlockSpec((tq, 256), lambda b, j: (b * nq + j, 0)),
        scratch_shapes=[pltpu.VMEM((MLA_HEADS, max(TK_MLA, c), tq), _F32),
                        pltpu.VMEM((MLA_HEADS, max(TK_MLA, c), tq), _BF)],
        compiler_params=_params(2),
        name="mla_attn",
    )(*args)


def _wa_kernel(sink_ref, q_ref, *refs, has_lat, seq_len, n_sub):
    if has_lat:
        kl_ref, vl_ref, kc_ref, vc_ref, o_ref, sc_ref, pc_ref, sl_ref, pl_ref, bias_ref = refs
    else:
        kc_ref, vc_ref, o_ref, sc_ref, pc_ref = refs
    tq = q_ref.shape[0] // n_sub
    n_ctx = kc_ref.shape[1]
    win_keys = tq + 2 * WINDOW
    step_q0 = pl.program_id(1) * (n_sub * tq)

    def rows_of(u):
        return pl.ds(pl.multiple_of(u * tq, tq), tq)

    def window_of(u):
        q0 = step_q0 + u * tq
        return q0, pl.multiple_of(jnp.clip(q0 - WINDOW, 0, seq_len - win_keys), LANES)

    def scores_stage(u):
        rows = rows_of(u)
        if has_lat:
            q0, start = window_of(u)
            kpos = start + lax.broadcasted_iota(jnp.int32, (win_keys, tq), 0)
            qpos = q0 + lax.broadcasted_iota(jnp.int32, (win_keys, tq), 1)
            bias_ref[...] = jnp.where(jnp.abs(kpos - qpos) <= WINDOW, 0.0, NEG)
        ms = []
        for hq in range(WA_HEADS):
            g = hq // (WA_HEADS // WA_KV_HEADS)
            qpair = q_ref[rows, g * LANES:(g + 1) * LANES]
            s_ctx = _dot_nt(kc_ref[hq], qpair)
            sc_ref[hq] = s_ctx
            m = jnp.maximum(jnp.max(s_ctx, axis=0, keepdims=True), sink_ref[hq] * LOG2E)
            if has_lat:
                s_loc = _dot_nt(kl_ref[hq, pl.ds(start, win_keys), :], qpair) + bias_ref[...]
                sl_ref[hq] = s_loc
                m = jnp.maximum(m, jnp.max(s_loc, axis=0, keepdims=True))
            ms.append(m)
        return tuple(ms)

    def softmax_stage(ms):
        for hq in range(WA_HEADS):
            for r0 in range(0, n_ctx, SM_ROWS):
                pc_ref[hq, r0:r0 + SM_ROWS, :] = _exp2_bf(sc_ref[hq, r0:r0 + SM_ROWS, :] - ms[hq])
            if has_lat:
                for r0 in range(0, win_keys, SM_ROWS):
                    pl_ref[hq, r0:r0 + SM_ROWS, :] = _exp2_bf(sl_ref[hq, r0:r0 + SM_ROWS, :] - ms[hq])

    def value_stage(u, ms):
        if has_lat:
            _, start = window_of(u)
        outs = []
        for hq in range(WA_HEADS):
            g = hq // (WA_HEADS // WA_KV_HEADS)
            vrows = slice(g * WA_HEAD_DIM, (g + 1) * WA_HEAD_DIM)
            acc = _dot(jnp.concatenate([vc_ref[vrows, :], jnp.ones((ONES_ROWS, n_ctx), _BF)], axis=0), pc_ref[hq])
            if has_lat:
                acc = acc + _dot(jnp.concatenate([vl_ref[vrows, pl.ds(start, win_keys)],
                                                  jnp.ones((ONES_ROWS, win_keys), _BF)], axis=0), pl_ref[hq])
            l = acc[WA_HEAD_DIM:WA_HEAD_DIM + 1] + jnp.exp2(sink_ref[hq] * LOG2E - ms[hq])
            outs.append(acc[:WA_HEAD_DIM] / l)
        o_ref[rows_of(u), :] = jnp.concatenate(outs, axis=0).T.astype(_BF)

    def body(u, state):
        ms_prev, ms_cur = state
        value_stage(u - 1, ms_prev)
        softmax_stage(ms_cur)
        return ms_cur, scores_stage(u + 1)

    ms_cur = scores_stage(0)
    softmax_stage(ms_cur)
    if n_sub > 1:
        state = lax.fori_loop(1, n_sub - 1, body, (ms_cur, scores_stage(1)))
        ms_prev, ms_cur = state
        value_stage(n_sub - 2, ms_prev)
        softmax_stage(ms_cur)
    value_stage(n_sub - 1, ms_cur)


def _wa_attn(sink, q, k_lat, vt_lat, k_ctx, vt_ctx, *, batch, tq, n_sub):
    t = q.shape[0]
    per_b = t // batch
    step_q = tq * n_sub
    nq = per_b // step_q
    c = k_ctx.shape[1] // batch
    has_lat = k_lat is not None
    in_specs = [pl.BlockSpec(memory_space=pltpu.SMEM),
                pl.BlockSpec((step_q, 256), lambda b, j: (b * nq + j, 0))]
    args = [sink, q]
    seq_len = 0
    if has_lat:
        seq_len = k_lat.shape[1] // batch
        in_specs += [pl.BlockSpec((WA_HEADS, seq_len, LANES), lambda b, j: (0, b, 0)),
                     pl.BlockSpec((128, seq_len), lambda b, j: (0, b))]
        args += [k_lat, vt_lat]
    in_specs += [pl.BlockSpec((WA_HEADS, c, LANES), lambda b, j: (0, b, 0)),
                 pl.BlockSpec((128, c), lambda b, j: (0, b))]
    args += [k_ctx, vt_ctx]
    win_keys = tq + 2 * WINDOW
    scratch = [pltpu.VMEM((WA_HEADS, c, tq), _F32), pltpu.VMEM((WA_HEADS, c, tq), _BF)]
    if has_lat:
        scratch += [pltpu.VMEM((WA_HEADS, win_keys, tq), _F32),
                    pltpu.VMEM((WA_HEADS, win_keys, tq), _BF),
                    pltpu.VMEM((win_keys, tq), _F32)]
    return pl.pallas_call(
        functools.partial(_wa_kernel, has_lat=has_lat, seq_len=seq_len, n_sub=n_sub),
        out_shape=jax.ShapeDtypeStruct((t, 256), _BF),
        grid=(batch, nq),
        in_specs=in_specs,
        out_specs=pl.BlockSpec((step_q, 256), lambda b, j: (b * nq + j, 0)),
        scratch_shapes=scratch,
        compiler_params=_params(2),
        name="wa_attn",
    )(*args)


HALO_SC = 8
HALO_CF = 16
CONV_ROWS = 64
SC_PHASES = ((HALO_SC - SC_K // 2) % SUBLANES, (HALO_SC + SC_K // 2) % SUBLANES)
CF_PHASES = tuple(range(1, SUBLANES))


def _mix_out_kernel(h_ref, mod_ref, oa_ref, oc_ref, bg_ref, cx_ref, ud_ref, *rest, halo, seq_tiles):
    if halo:
        cxp_ref, cxn_ref, udp_ref, udn_ref = rest[:4]
        rest = rest[4:]
    (wsc_ref, wcf_ref, bcf_ref, gln_ref, bln_ref, wout_ref, o_ref,
     xsc_ref, xcf_ref, ssc_ref, scf_ref, ob_ref, od_ref) = rest
    tm = h_ref.shape[0]
    width = cx_ref.shape[1]

    if halo:
        i = pl.program_id(0) % seq_tiles
        has_prev = (i != 0).astype(_F32)
        has_next = (i != seq_tiles - 1).astype(_F32)
        xsc_ref[0:HALO_SC, :] = cxp_ref[...] * has_prev
        xsc_ref[HALO_SC + tm:, :] = cxn_ref[...] * has_next
        xcf_ref[0:HALO_CF, :] = udp_ref[...] * has_prev
        xcf_ref[HALO_CF + tm:, :] = udn_ref[...] * has_next
    else:
        xsc_ref[0:HALO_SC, :] = jnp.zeros((HALO_SC, width), _F32)
        xsc_ref[HALO_SC + tm:, :] = jnp.zeros((HALO_SC, width), _F32)
        xcf_ref[0:HALO_CF, :] = jnp.zeros((HALO_CF, width), _F32)
        xcf_ref[HALO_CF + tm:, :] = jnp.zeros((HALO_CF, width), _F32)
    xsc_ref[HALO_SC:HALO_SC + tm, :] = cx_ref[...]
    xcf_ref[HALO_CF:HALO_CF + tm, :] = ud_ref[...]

    for r in range(1, SUBLANES):
        scf_ref[r - 1] = xcf_ref[r:r + scf_ref.shape[1], :]
    for n, r in enumerate(SC_PHASES):
        ssc_ref[n] = xsc_ref[r:r + ssc_ref.shape[1], :]

    def tap(x_ref, copies_ref, phases, row):
        r = row % SUBLANES
        base = row - r
        if r == 0:
            return x_ref[base:base + CONV_ROWS, :]
        return copies_ref[phases.index(r), base:base + CONV_ROWS, :]

    for r0 in range(0, tm, CONV_ROWS):
        acc = jnp.zeros((CONV_ROWS, width), _F32)
        for k in range(SC_K):
            row = HALO_SC + r0 + k - SC_K // 2
            acc = acc + tap(xsc_ref, ssc_ref, SC_PHASES, row) * wsc_ref[k:k + 1, :]
        ob_ref[r0:r0 + CONV_ROWS, :] = (bg_ref[r0:r0 + CONV_ROWS, :] * acc).astype(_BF)

        acc = jnp.zeros((CONV_ROWS, width), _F32)
        for k in range(CF_K):
            row = HALO_CF + r0 + k - CF_K // 2
            acc = acc + tap(xcf_ref, scf_ref, CF_PHASES, row) * wcf_ref[k:k + 1, :]
        u = acc + bcf_ref[...]
        mu = jnp.mean(u, axis=-1, keepdims=True)
        uc = u - mu
        var = jnp.mean(uc * uc, axis=-1, keepdims=True)
        y = uc * lax.rsqrt(var + EPS) * gln_ref[...] + bln_ref[...]
        od_ref[r0:r0 + CONV_ROWS, :] = (y * jax.nn.sigmoid(y)).astype(_BF)

    mixed = jnp.concatenate([oa_ref[...], ob_ref[...], oc_ref[...], od_ref[...]], axis=1)
    o_ref[...] = h_ref[...] + mod_ref[0, 5:6, :] * _dot(mixed, wout_ref[0])


def _mix_out(h, mod, oa, oc, bg, cx, ud, w, w_out, *, layer, tm, tiles_per_mod, mod_base, seq_tiles):
    t, d = h.shape
    halo = seq_tiles > 1
    if tiles_per_mod is None:
        mod_map = lambda i: (mod_base, 0, 0)
    else:
        mod_map = lambda i: (i // tiles_per_mod, 0, 0)
    row = lambda width: pl.BlockSpec((tm, width), lambda i: (i, 0))
    in_specs = [row(d), pl.BlockSpec((1, N_MOD, d), mod_map), row(256), row(256), row(256), row(256), row(256)]
    args = [h, mod, oa, oc, bg, cx, ud]
    if halo:
        nsc = tm // HALO_SC
        ncf = tm // HALO_CF
        last_sc = t // HALO_SC - 1
        last_cf = t // HALO_CF - 1
        in_specs += [
            pl.BlockSpec((HALO_SC, 256), lambda i: (jnp.maximum(i * nsc - 1, 0), 0)),
            pl.BlockSpec((HALO_SC, 256), lambda i: (jnp.minimum((i + 1) * nsc, last_sc), 0)),
            pl.BlockSpec((HALO_CF, 256), lambda i: (jnp.maximum(i * ncf - 1, 0), 0)),
            pl.BlockSpec((HALO_CF, 256), lambda i: (jnp.minimum((i + 1) * ncf, last_cf), 0)),
        ]
        args += [cx, cx, ud, ud]
    in_specs += [_const_spec((SC_K, 256)), _const_spec((CF_K, 256)), _const_spec((1, 256)),
                 _const_spec((1, 256)), _const_spec((1, 256)), _layer_spec((d, d), layer)]
    args += [w["w_sc"], w["w_cf"], w["b_cf"], w["g_ln"], w["b_ln"], w_out]
    return pl.pallas_call(
        functools.partial(_mix_out_kernel, halo=halo, seq_tiles=seq_tiles),
        out_shape=jax.ShapeDtypeStruct((t, d), _F32),
        grid=(t // tm,),
        in_specs=in_specs,
        out_specs=row(d),
        scratch_shapes=[pltpu.VMEM((tm + 2 * HALO_SC, 256), _F32),
                        pltpu.VMEM((tm + 2 * HALO_CF, 256), _F32),
                        pltpu.VMEM((len(SC_PHASES), tm + HALO_SC, 256), _F32),
                        pltpu.VMEM((len(CF_PHASES), tm + 2 * HALO_CF - SUBLANES, 256), _F32),
                        pltpu.VMEM((tm, 256), _BF),
                        pltpu.VMEM((tm, 256), _BF)],
        compiler_params=_params(1),
        name="mix_out",
    )(*args)


def _swap_halves(w, d_rot):
    m = d_rot // 4
    lead = w.shape[:-1]
    return jnp.flip(w.reshape(lead + (-1, 2, m)), axis=-2).reshape(w.shape)


def _rope_table(seq, d_rot, lanes_before):
    m = d_rot // 4
    d_ax = d_rot // 2
    rows = seq // GRID_W
    inv = ROPE_BASE ** (-jnp.arange(0, d_ax, 2, dtype=_F32) / d_ax)
    ar = jnp.arange(rows, dtype=_F32)[:, None] * inv[None, :]
    ac = jnp.arange(GRID_W, dtype=_F32)[:, None] * inv[None, :]
    reps = (LANES - lanes_before) // d_rot if lanes_before == 0 else 1
    pad = LANES - lanes_before - reps * d_rot

    def table(n, group, fill):
        return jnp.concatenate([jnp.full((n, lanes_before), fill, _F32)] + [jnp.concatenate(group, axis=1)] * reps
                               + [jnp.full((n, pad), fill, _F32)], axis=1)

    zr = jnp.zeros((rows, m), _F32)
    zc = jnp.zeros((GRID_W, m), _F32)
    c_row = table(rows, [jnp.cos(ar), jnp.cos(ar), zr, zr], 0.0)
    s_row = table(rows, [-jnp.sin(ar), jnp.sin(ar), zr, zr], 0.0)
    c_col = table(GRID_W, [zc, zc, jnp.cos(ac), jnp.cos(ac)], 1.0)
    s_col = table(GRID_W, [zc, zc, -jnp.sin(ac), jnp.sin(ac)], 0.0)
    return c_row, c_col, s_row, s_col


def _pack_layer(p, l):
    offs = np.concatenate([[0], np.cumsum(IN_SIZES)])
    o_cq, o_ckv, o_kr, o_sc, o_wq, o_wkv, o_cf, o_end = (int(v) for v in offs)
    scale_a = float((MLA_NOPE + MLA_ROPE) ** -0.5 * LOG2E)
    scale_w = float(WA_HEAD_DIM ** -0.5 * LOG2E)
    w = p["w_in"][l]
    d = w.shape[0]
    zeros = lambda n: jnp.zeros((d, n), _F32)

    kr = w[:, o_kr:o_sc]
    wq = w[:, o_wq:o_wkv] * scale_w
    nk = WA_KV_HEADS * WA_HEAD_DIM
    wk = w[:, o_wkv:o_wkv + nk]
    pad_r = LANES - MLA_NOPE - MLA_ROPE
    w_in = jnp.concatenate([
        w[:, o_cq:o_ckv], zeros(256 - MLA_Q_RANK),
        zeros(MLA_NOPE), kr, zeros(pad_r), w[:, o_ckv:o_kr],
        w[:, o_sc:o_wq],
        wq,
        w[:, o_cf:o_end],
        wk, w[:, o_wkv + nk:o_cf],
    ], axis=1).astype(_BF)

    wu = p["w_mla_uq"][l].reshape(MLA_Q_RANK, MLA_HEADS, MLA_NOPE + MLA_ROPE) * scale_a
    nope, rope = wu[..., :MLA_NOPE], wu[..., MLA_NOPE:]
    zq = lambda n: jnp.zeros((MLA_Q_RANK, MLA_HEADS, n), _F32)
    w_uq = jnp.concatenate([
        jnp.concatenate([nope, rope, zq(pad_r)], axis=-1).reshape(MLA_Q_RANK, MLA_HEADS * LANES),
        jnp.concatenate([zq(MLA_NOPE), _swap_halves(rope, MLA_ROPE), zq(pad_r)],
                        axis=-1).reshape(MLA_Q_RANK, MLA_HEADS * LANES),
    ], axis=1)
    w_uq = jnp.pad(w_uq, ((0, 256 - MLA_Q_RANK), (0, 0))).astype(_BF)
    g_q = jnp.pad(p["g_mla_q"][l], (0, 256 - MLA_Q_RANK)).reshape(1, 256)

    wkv = p["w_mla_ukv"][l].reshape(MLA_KV_RANK, MLA_HEADS, MLA_NOPE + MLA_V)
    k_nope = jnp.concatenate(
        [wkv[..., :MLA_NOPE], jnp.zeros((MLA_KV_RANK, MLA_HEADS, LANES - MLA_NOPE), _F32)], axis=-1)
    w_ukv = jnp.concatenate([k_nope.reshape(MLA_KV_RANK, MLA_HEADS * LANES),
                             wkv[..., MLA_NOPE:].reshape(MLA_KV_RANK, MLA_HEADS * MLA_V)], axis=1).astype(_BF)

    return dict(
        w_in=w_in, w_uq=w_uq, g_q=g_q, w_ukv=w_ukv, g_kv=p["g_mla_kv"][l].reshape(1, MLA_KV_RANK),
        w_sc=p["w_sc_conv"][l], w_cf=p["w_cf_conv"][l], b_cf=p["b_cf_conv"][l].reshape(1, -1),
        g_ln=p["g_cf_ln"][l].reshape(1, -1), b_ln=p["b_cf_ln"][l].reshape(1, -1),
        g_ffn1=p["g_ffn1"][l].reshape(1, -1), g_mix=p["g_mix"][l].reshape(1, -1),
        g_ffn2=p["g_ffn2"][l].reshape(1, -1),
        sink=p["wa_sink"][l],
    )


def kernel(x, c, ctx, c_ctx, w_mod, b_mod, g_ffn1, w1_gate, w1_up, w1_down, g_mix, w_in, g_mla_q, w_mla_uq, g_mla_kv, w_mla_ukv, w_sc_conv, wa_sink, w_cf_conv, b_cf_conv, g_cf_ln, b_cf_ln, w_out, g_ffn2, w2_gate, w2_up, w2_down, g_final):
    b, s, d = x.shape
    n_ctx = ctx.shape[1]
    depth = w_mod.shape[0]
    p = dict(w_in=w_in, g_mla_q=g_mla_q, w_mla_uq=w_mla_uq, g_mla_kv=g_mla_kv, w_mla_ukv=w_mla_ukv,
             w_sc_conv=w_sc_conv, wa_sink=wa_sink, w_cf_conv=w_cf_conv, b_cf_conv=b_cf_conv,
             g_cf_ln=g_cf_ln, b_cf_ln=b_cf_ln, w_out=w_out, g_ffn1=g_ffn1, g_mix=g_mix, g_ffn2=g_ffn2,
             w1_gate=w1_gate, w1_up=w1_up, w1_down=w1_down, w2_gate=w2_gate, w2_up=w2_up, w2_down=w2_down)

    tm = TM_LAT
    seq_tiles = s // tm
    tabs = _rope_table(s, MLA_ROPE, MLA_NOPE) + _rope_table(s, WA_HEAD_DIM, 0)

    mod_rows = 8
    cs = jnp.concatenate([c, c_ctx[None, :], jnp.zeros((mod_rows - b - 1, d), _F32)], axis=0)
    mod_all = _modulation(cs, w_mod, b_mod).reshape(depth, mod_rows, N_MOD, d)

    lat = dict(tm=tm, tiles_per_mod=seq_tiles, mod_base=0)
    cxt = dict(tm=n_ctx, tiles_per_mod=None, mod_base=b)

    w1 = (_cast_bf16(w1_gate), _cast_bf16(w1_up), _cast_bf16(w1_down))
    w2 = (_cast_bf16(w2_gate), _cast_bf16(w2_up), _cast_bf16(w2_down))
    w_out_b = _cast_bf16(w_out)

    h_lat = x.reshape(b * s, d)
    h_ctx = ctx.reshape(b * n_ctx, d)
    for l in range(depth):
        last = l == depth - 1
        w = _pack_layer(p, l)
        mod = mod_all[l]

        h_lat = _ffn(h_lat, mod, w["g_ffn1"], *w1, layer=l, which=0, n_sub=FFN_SUBTILES, **lat)
        h_ctx = _ffn(h_ctx, mod, w["g_ffn1"], *w1, layer=l, which=0, **cxt)

        qa, ka, vat, bg, cx, qw, kw, vwt, ud = _in_proj(h_lat, mod, w["g_mix"], w, tabs, seq_tiles=seq_tiles, **lat)
        qa_c, ka_c, vat_c, bg_c, cx_c, qw_c, kw_c, vwt_c, ud_c = _in_proj(
            h_ctx, mod, w["g_mix"], w, None, seq_tiles=1, **cxt)

        oa = _mla_attn(qa, ka, vat, ka_c, vat_c, batch=b, tq=TQ_MLA)
        oc = _wa_attn(w["sink"], qw, kw, vwt, kw_c, vwt_c, batch=b, tq=TQ_WA, n_sub=WA_SUBTILES)
        h_lat = _mix_out(h_lat, mod, oa, oc, bg, cx, ud, w, w_out_b, layer=l, seq_tiles=seq_tiles, **lat)

        if not last:
            oa_c = _mla_attn(qa_c, None, None, ka_c, vat_c, batch=b, tq=n_ctx)
            oc_c = _wa_attn(w["sink"], qw_c, None, None, kw_c, vwt_c, batch=b, tq=n_ctx, n_sub=1)
            h_ctx = _mix_out(h_ctx, mod, oa_c, oc_c, bg_c, cx_c, ud_c, w, w_out_b, layer=l, seq_tiles=1, **cxt)
            h_ctx = _ffn(h_ctx, mod, w["g_ffn2"], *w2, layer=l, which=2, **cxt)

        h_lat = _ffn(h_lat, mod, w["g_ffn2"], *w2, layer=l, which=2,
                     g_final=g_final.reshape(1, d) if last else None, n_sub=FFN_SUBTILES, **lat)

    return h_lat.reshape(b, s, d)
```

```python
import functools

import numpy as np
import jax
import jax.numpy as jnp
from jax import lax
from jax.experimental import pallas as pl
from jax.experimental.pallas import tpu as pltpu

GRID_W = 64
ROPE_BASE = 10000.0
EPS = 1e-6
NEG = -1e30
N_MOD = 9
LOG2E = 1.4426950408889634

MLA_HEADS = 4
MLA_NOPE = 64
MLA_ROPE = 32
MLA_V = 64
MLA_Q_RANK = 192
MLA_KV_RANK = 128
SC_WIDTH = 256
SC_K = 3
WA_HEADS = 4
WA_KV_HEADS = 2
WA_HEAD_DIM = 64
WINDOW = 128
CF_WIDTH = 256
CF_K = 31
IN_SIZES = (MLA_Q_RANK, MLA_KV_RANK, MLA_ROPE, 3 * SC_WIDTH,
            WA_HEADS * WA_HEAD_DIM, 2 * WA_KV_HEADS * WA_HEAD_DIM, 2 * CF_WIDTH)

LANES = 128
SUBLANES = 8
MXU_N = 256
SM_ROWS = 32
ONES_ROWS = 16

_BF = jnp.bfloat16
_F32 = jnp.float32

SEG_CQ = 0
SEG_KRC = 256
SEG_SC = 512
SEG_WQ = 1280
SEG_CF = 1536
SEG_WKV = 2048
N_PACK = 2304

TM_LAT = 512
TQ_MLA = 512
TK_MLA = 1024
TQ_WA = 256
WA_SUBTILES = 8
FF_CHUNK = 256
CAST_BLOCK_BYTES = 12 * 1024 * 1024
FFN_SUBTILES = 2
VMEM_LIMIT = 52 * 1024 * 1024


def _params(n_axes, flags=None):
    return pltpu.CompilerParams(dimension_semantics=("arbitrary",) * n_axes,
                                vmem_limit_bytes=VMEM_LIMIT, flags=flags)


def _const_spec(shape):
    nd = len(shape)
    return pl.BlockSpec(shape, lambda *_: (0,) * nd, pipeline_mode=pl.Buffered(1))


def _layer_spec(shape, layer):
    return pl.BlockSpec((1,) + tuple(shape), lambda *_: (layer, 0, 0), pipeline_mode=pl.Buffered(1))


def _cast_kernel(x_ref, o_ref):
    o_ref[...] = x_ref[...].astype(o_ref.dtype)


def _cast_bf16(w):
    depth, rows, cols = w.shape
    tr = rows
    while tr * cols * 4 > CAST_BLOCK_BYTES:
        tr //= 2
    return pl.pallas_call(
        _cast_kernel,
        out_shape=jax.ShapeDtypeStruct(w.shape, _BF),
        grid=(depth, rows // tr),
        in_specs=[pl.BlockSpec((1, tr, cols), lambda l, i: (l, i, 0))],
        out_specs=pl.BlockSpec((1, tr, cols), lambda l, i: (l, i, 0)),
        compiler_params=_params(2),
        name="cast_bf16",
    )(w)


def _dot(a, b):
    return jnp.dot(a, b, preferred_element_type=_F32)


def _dot_nt(a, b):
    return lax.dot_general(a, b, (((1,), (1,)), ((), ())), preferred_element_type=_F32)


def _exp2_bf(d):
    return jnp.exp2(d).astype(_BF)


def _swap_lanes(x, m):
    lane = lax.broadcasted_iota(jnp.int32, x.shape, 1)
    low = (lane % (2 * m)) < m
    return jnp.where(low, pltpu.roll(x, LANES - m, axis=1), pltpu.roll(x, m, axis=1))


def _norm_mod(x, g, shift, scale):
    y = x * lax.rsqrt(jnp.mean(x * x, axis=-1, keepdims=True) + EPS) * g
    return y * (1.0 + scale) + shift


def _mod_kernel(c_ref, w_ref, b_ref, o_ref):
    c = c_ref[...]
    a = c * jax.nn.sigmoid(c)
    rows = a.shape[0]
    a_hi = a.astype(_BF).astype(_F32)
    lhs = jnp.concatenate([a_hi, a - a_hi], axis=0).astype(_BF)
    w = w_ref[0]
    w_hi = w.astype(_BF)
    w_lo = (w - w_hi.astype(_F32)).astype(_BF)
    r = _dot(lhs, w_hi) + _dot(lhs, w_lo)
    o_ref[0] = r[:rows] + r[rows:] + b_ref[0]


def _modulation(cs, w_mod, b_mod):
    depth, d, n = w_mod.shape
    tn = 1024
    rows = cs.shape[0]
    return pl.pallas_call(
        _mod_kernel,
        out_shape=jax.ShapeDtypeStruct((depth, rows, n), _F32),
        grid=(depth, n // tn),
        in_specs=[pl.BlockSpec((rows, d), lambda l, j: (0, 0)),
                  pl.BlockSpec((1, d, tn), lambda l, j: (l, 0, j)),
                  pl.BlockSpec((1, 1, tn), lambda l, j: (l, 0, j))],
        out_specs=pl.BlockSpec((1, rows, tn), lambda l, j: (l, 0, j)),
        compiler_params=_params(2),
        name="modulation",
    )(cs, w_mod, b_mod.reshape(depth, 1, n))


def _ffn_kernel(h_ref, mod_ref, g_ref, wg_ref, wu_ref, wd_ref, *rest, which, final, n_sub):
    if final:
        gf_ref, o_ref, a_ref = rest
    else:
        o_ref, a_ref = rest
    shift = mod_ref[0, 3 * which:3 * which + 1, :]
    scale = mod_ref[0, 3 * which + 1:3 * which + 2, :]
    gate = mod_ref[0, 3 * which + 2:3 * which + 3, :]
    d_ff = wg_ref.shape[2]
    rows_per = h_ref.shape[0] // n_sub
    for sub in range(n_sub):
        rows = slice(sub * rows_per, (sub + 1) * rows_per)
        x = h_ref[rows, :]
        xb = _norm_mod(x, g_ref[...], shift, scale).astype(_BF)
        for j in range(d_ff // FF_CHUNK):
            sl = slice(j * FF_CHUNK, (j + 1) * FF_CHUNK)
            gt = _dot(xb, wg_ref[0, :, sl])
            up = _dot(xb, wu_ref[0, :, sl])
            a_ref[rows, sl] = (gt * jax.nn.sigmoid(gt) * up).astype(_BF)
        y = x + 0.5 * gate * _dot(a_ref[rows, :], wd_ref[0])
        if final:
            y = y * lax.rsqrt(jnp.mean(y * y, axis=-1, keepdims=True) + EPS) * gf_ref[...]
        o_ref[rows, :] = y


def _ffn(h, mod, g, wg, wu, wd, *, layer, which, tm, tiles_per_mod, mod_base, g_final=None, n_sub=1):
    t, d = h.shape
    d_ff = wg.shape[2]
    final = g_final is not None
    tm = tm * n_sub
    if tiles_per_mod is None:
        mod_map = lambda i: (mod_base, 0, 0)
    else:
        mod_map = lambda i: (i // (tiles_per_mod // n_sub), 0, 0)
    in_specs = [pl.BlockSpec((tm, d), lambda i: (i, 0)),
                pl.BlockSpec((1, N_MOD, d), mod_map),
                _const_spec((1, d)),
                _layer_spec((d, d_ff), layer), _layer_spec((d, d_ff), layer), _layer_spec((d_ff, d), layer)]
    args = [h, mod, g, wg, wu, wd]
    if final:
        in_specs.append(_const_spec((1, d)))
        args.append(g_final)
    return pl.pallas_call(
        functools.partial(_ffn_kernel, which=which, final=final, n_sub=n_sub),
        out_shape=jax.ShapeDtypeStruct((t, d), _F32),
        grid=(t // tm,),
        in_specs=in_specs,
        out_specs=pl.BlockSpec((tm, d), lambda i: (i, 0)),
        scratch_shapes=[pltpu.VMEM((tm, d_ff), _BF)],
        compiler_params=_params(1),
        name="ffn",
    )(*args)


def _in_proj_kernel(h_ref, mod_ref, g_ref, win_ref, gq_ref, wuq_ref, gkv_ref, wukv_ref, *rest, rope):
    if rope:
        tab_refs = rest[:8]
        rest = rest[8:]
    qa_ref, ka_ref, vat_ref, bg_ref, cx_ref, qw_ref, kw_ref, vwt_ref, ud_ref, z_ref = rest
    if rope:
        def expand(row_ref, col_ref, row_lanes):
            col = col_ref[...]
            return jnp.concatenate([jnp.where(row_lanes, row_ref[r:r + 1, :], col)
                                    for r in range(h_ref.shape[0] // GRID_W)], axis=0)

        lane = lax.broadcasted_iota(jnp.int32, (GRID_W, LANES), 1)
        rows_a = jnp.abs(2 * (lane - MLA_NOPE) - (MLA_ROPE // 2 - 1)) < MLA_ROPE // 2
        rows_w = (lane % WA_HEAD_DIM) < WA_HEAD_DIM // 2
        ca, sa = expand(tab_refs[0], tab_refs[1], rows_a), expand(tab_refs[2], tab_refs[3], rows_a)
        cw, sw = expand(tab_refs[4], tab_refs[5], rows_w), expand(tab_refs[6], tab_refs[7], rows_w)

    x = h_ref[...]
    xb = _norm_mod(x, g_ref[...], mod_ref[0, 3:4, :], mod_ref[0, 4:5, :]).astype(_BF)

    z_ref[...] = _dot(xb, win_ref[...])

    def seg(lo, width):
        return z_ref[:, lo:lo + width]

    cq = seg(SEG_CQ, 256)
    cqn = cq * lax.rsqrt(jnp.sum(cq * cq, axis=-1, keepdims=True) * (1.0 / MLA_Q_RANK) + EPS) * gq_ref[...]
    qq = _dot(cqn.astype(_BF), wuq_ref[...])
    q = qq[:, :MLA_HEADS * LANES]
    if rope:
        q = (q * jnp.concatenate([ca] * MLA_HEADS, axis=1)
             + qq[:, MLA_HEADS * LANES:] * jnp.concatenate([sa] * MLA_HEADS, axis=1))
    qa_ref[...] = q.astype(_BF)

    krc = seg(SEG_KRC, 256)
    ckv = krc[:, LANES:]
    ckvn = ckv * lax.rsqrt(jnp.mean(ckv * ckv, axis=-1, keepdims=True) + EPS) * gkv_ref[...]
    kv = _dot(ckvn.astype(_BF), wukv_ref[...])
    kr = krc[:, :LANES]
    if rope:
        kr = kr * ca + _swap_lanes(kr, MLA_ROPE // 4) * sa
    for h in range(MLA_HEADS):
        ka_ref[h] = (kv[:, h * LANES:(h + 1) * LANES] + kr).astype(_BF)
    vat_ref[...] = kv[:, MLA_HEADS * LANES:].T.astype(_BF)

    sc = seg(SEG_SC, 768)
    bg_ref[...] = sc[:, :256]
    cx_ref[...] = sc[:, 256:512] * sc[:, 512:]

    qw = seg(SEG_WQ, 256)
    wkv = seg(SEG_WKV, 256)
    kw = wkv[:, :LANES]
    if rope:
        m_w = WA_HEAD_DIM // 4
        qw = jnp.concatenate([qw[:, :LANES] * cw + _swap_lanes(qw[:, :LANES], m_w) * sw,
                              qw[:, LANES:] * cw + _swap_lanes(qw[:, LANES:], m_w) * sw], axis=1)
        kw = kw * cw + _swap_lanes(kw, m_w) * sw
    qw_ref[...] = qw.astype(_BF)
    lane = lax.broadcasted_iota(jnp.int32, kw.shape, 1)
    low = lane < WA_HEAD_DIM
    kw_r = pltpu.roll(kw, WA_HEAD_DIM, axis=1)
    zero = jnp.zeros_like(kw)
    variants = (jnp.where(low, kw, zero), jnp.where(low, zero, kw_r),
                jnp.where(low, kw_r, zero), jnp.where(low, zero, kw))
    for hq in range(WA_HEADS):
        kw_ref[hq] = variants[hq].astype(_BF)
    vwt_ref[...] = wkv[:, LANES:].T.astype(_BF)

    cf = seg(SEG_CF, 512)
    ud_ref[...] = cf[:, :256] * jax.nn.sigmoid(cf[:, 256:])


def _in_proj(h, mod, g, w, tabs, *, tm, tiles_per_mod, mod_base, seq_tiles):
    t, d = h.shape
    rope = tabs is not None
    if tiles_per_mod is None:
        mod_map = lambda i: (mod_base, 0, 0)
    else:
        mod_map = lambda i: (i // tiles_per_mod, 0, 0)
    in_specs = [pl.BlockSpec((tm, d), lambda i: (i, 0)),
                pl.BlockSpec((1, N_MOD, d), mod_map),
                _const_spec((1, d)),
                _const_spec((d, N_PACK)),
                _const_spec((1, 256)), _const_spec((256, 1024)),
                _const_spec((1, 128)), _const_spec((128, 768))]
    args = [h, mod, g, w["w_in"], w["g_q"], w["w_uq"], w["g_kv"], w["w_ukv"]]
    if rope:
        row_tab = pl.BlockSpec((tm // GRID_W, LANES), lambda i: (i % seq_tiles, 0))
        in_specs += [row_tab, _const_spec((GRID_W, LANES))] * 4
        args += list(tabs)
    row = lambda width: pl.BlockSpec((tm, width), lambda i: (i, 0))
    col = lambda height: pl.BlockSpec((height, tm), lambda i: (0, i))
    out_shape = (
        jax.ShapeDtypeStruct((t, 512), _BF),
        jax.ShapeDtypeStruct((MLA_HEADS, t, LANES), _BF),
        jax.ShapeDtypeStruct((256, t), _BF),
        jax.ShapeDtypeStruct((t, 256), _F32),
        jax.ShapeDtypeStruct((t, 256), _F32),
        jax.ShapeDtypeStruct((t, 256), _BF),
        jax.ShapeDtypeStruct((WA_HEADS, t, LANES), _BF),
        jax.ShapeDtypeStruct((128, t), _BF),
        jax.ShapeDtypeStruct((t, 256), _F32),
    )
    heads = lambda n: pl.BlockSpec((n, tm, LANES), lambda i: (0, i, 0))
    out_specs = (row(512), heads(MLA_HEADS), col(256), row(256), row(256), row(256), heads(WA_HEADS),
                 col(128), row(256))
    return pl.pallas_call(
        functools.partial(_in_proj_kernel, rope=rope),
        out_shape=out_shape,
        grid=(t // tm,),
        in_specs=in_specs,
        out_specs=out_specs,
        scratch_shapes=[pltpu.VMEM((tm, N_PACK), _F32)],
        compiler_params=_params(1),
        name="in_proj",
    )(*args)


def _mla_kernel(q_ref, *refs, has_lat, n_lat_tiles):
    if has_lat:
        kl_ref, vl_ref, kc_ref, vc_ref, o_ref, s_ref, p_ref = refs
    else:
        kc_ref, vc_ref, o_ref, s_ref, p_ref = refs
    tq = q_ref.shape[0]
    n_ctx = kc_ref.shape[1]

    def scores_stage(k_ref, key_slice, tk):
        tile_max = []
        for h in range(MLA_HEADS):
            s = _dot_nt(k_ref[h, key_slice, :],
                        q_ref[:, h * LANES:(h + 1) * LANES])
            s_ref[h, 0:tk, :] = s
            tile_max.append(jnp.max(s, axis=0, keepdims=True))
        return tuple(tile_max)

    def softmax_stage(tk, tile_max, ms):
        new_ms, alphas = [], []
        for h in range(MLA_HEADS):
            m_new = jnp.maximum(ms[h], tile_max[h])
            new_ms.append(m_new)
            alphas.append(jnp.exp2(ms[h] - m_new))
            for r0 in range(0, tk, SM_ROWS):
                p_ref[h, r0:r0 + SM_ROWS, :] = _exp2_bf(s_ref[h, r0:r0 + SM_ROWS, :] - m_new)
        return tuple(new_ms), tuple(alphas)

    def value_stage(vt_ref, key_slice, tk, alphas, accs):
        ones = jnp.ones((ONES_ROWS, tk), _BF)
        new = []
        for h in range(MLA_HEADS):
            v1 = jnp.concatenate([vt_ref[h * MLA_V:(h + 1) * MLA_V, key_slice], ones], axis=0)
            new.append(alphas[h] * accs[h] + _dot(v1, p_ref[h, 0:tk, :]))
        return tuple(new)

    ms = tuple(jnp.full((1, tq), NEG, _F32) for _ in range(MLA_HEADS))
    accs = tuple(jnp.zeros((MLA_V + ONES_ROWS, tq), _F32) for _ in range(MLA_HEADS))
    if has_lat:
        def lat(t):
            return pl.ds(pl.multiple_of(t * TK_MLA, TK_MLA), TK_MLA)

        def body(k, state, ctx_next):
            tile_max, ms, accs = state
            ms, alphas = softmax_stage(TK_MLA, tile_max, ms)
            if ctx_next:
                tile_max = scores_stage(kc_ref, slice(None), n_ctx)
            else:
                tile_max = scores_stage(kl_ref, lat(k + 1), TK_MLA)
            return tile_max, ms, value_stage(vl_ref, lat(k), TK_MLA, alphas, accs)

        state = (scores_stage(kl_ref, lat(0), TK_MLA), ms, accs)
        state = lax.fori_loop(0, n_lat_tiles - 1, functools.partial(body, ctx_next=False), state)
        tile_max, ms, accs = body(n_lat_tiles - 1, state, True)
    else:
        tile_max = scores_stage(kc_ref, slice(None), n_ctx)
    ms, alphas = softmax_stage(n_ctx, tile_max, ms)
    accs = value_stage(vc_ref, slice(None), n_ctx, alphas, accs)
    outs = [acc[:MLA_V] / acc[MLA_V:MLA_V + 1] for acc in accs]
    o_ref[...] = jnp.concatenate(outs, axis=0).T.astype(_BF)


def _mla_attn(q, k_lat, vt_lat, k_ctx, vt_ctx, *, batch, tq):
    t = q.shape[0]
    per_b = t // batch
    nq = per_b // tq
    c = k_ctx.shape[1] // batch
    has_lat = k_lat is not None
    in_specs = [pl.BlockSpec((tq, 512), lambda b, j: (b * nq + j, 0))]
    args = [q]
    n_lat_tiles = 0
    if has_lat:
        s = k_lat.shape[1] // batch
        n_lat_tiles = s // TK_MLA
        in_specs += [pl.BlockSpec((MLA_HEADS, s, LANES), lambda b, j: (0, b, 0)),
                     pl.BlockSpec((256, s), lambda b, j: (0, b))]
        args += [k_lat, vt_lat]
    in_specs += [pl.BlockSpec((MLA_HEADS, c, LANES), lambda b, j: (0, b, 0)),
                 pl.BlockSpec((256, c), lambda b, j: (0, b))]
    args += [k_ctx, vt_ctx]
    return pl.pallas_call(
        functools.partial(_mla_kernel, has_lat=has_lat, n_lat_tiles=n_lat_tiles),
        out_shape=jax.ShapeDtypeStruct((t, 256), _BF),
        grid=(batch, nq),
        in_specs=in_specs,
        out_specs=pl.BlockSpec((tq, 256), lambda b, j: (b * nq + j, 0)),
        scratch_shapes=[pltpu.VMEM((MLA_HEADS, max(TK_MLA, c), tq), _F32),
                        pltpu.VMEM((MLA_HEADS, max(TK_MLA, c), tq), _BF)],
        compiler_params=_params(2),
        name="mla_attn",
    )(*args)


def _wa_kernel(sink_ref, q_ref, *refs, has_lat, seq_len, n_sub):
    if has_lat:
        kl_ref, vl_ref, kc_ref, vc_ref, o_ref, sc_ref, pc_ref, sl_ref, pl_ref, bias_ref = refs
    else:
        kc_ref, vc_ref, o_ref, sc_ref, pc_ref = refs
    tq = q_ref.shape[0] // n_sub
    n_ctx = kc_ref.shape[1]
    win_keys = tq + 2 * WINDOW
    step_q0 = pl.program_id(1) * (n_sub * tq)

    def rows_of(u):
        return pl.ds(pl.multiple_of(u * tq, tq), tq)

    def window_of(u):
        q0 = step_q0 + u * tq
        return q0, pl.multiple_of(jnp.clip(q0 - WINDOW, 0, seq_len - win_keys), LANES)

    def scores_stage(u):
        rows = rows_of(u)
        if has_lat:
            q0, start = window_of(u)
            kpos = start + lax.broadcasted_iota(jnp.int32, (win_keys, tq), 0)
            qpos = q0 + lax.broadcasted_iota(jnp.int32, (win_keys, tq), 1)
            bias_ref[...] = jnp.where(jnp.abs(kpos - qpos) <= WINDOW, 0.0, NEG)
        ms = []
        for hq in range(WA_HEADS):
            g = hq // (WA_HEADS // WA_KV_HEADS)
            qpair = q_ref[rows, g * LANES:(g + 1) * LANES]
            s_ctx = _dot_nt(kc_ref[hq], qpair)
            sc_ref[hq] = s_ctx
            m = jnp.maximum(jnp.max(s_ctx, axis=0, keepdims=True), sink_ref[hq] * LOG2E)
            if has_lat:
                s_loc = _dot_nt(kl_ref[hq, pl.ds(start, win_keys), :], qpair) + bias_ref[...]
                sl_ref[hq] = s_loc
                m = jnp.maximum(m, jnp.max(s_loc, axis=0, keepdims=True))
            ms.append(m)
        return tuple(ms)

    def softmax_stage(ms):
        for hq in range(WA_HEADS):
            for r0 in range(0, n_ctx, SM_ROWS):
                pc_ref[hq, r0:r0 + SM_ROWS, :] = _exp2_bf(sc_ref[hq, r0:r0 + SM_ROWS, :] - ms[hq])
            if has_lat:
                for r0 in range(0, win_keys, SM_ROWS):
                    pl_ref[hq, r0:r0 + SM_ROWS, :] = _exp2_bf(sl_ref[hq, r0:r0 + SM_ROWS, :] - ms[hq])

    def value_stage(u, ms):
        if has_lat:
            _, start = window_of(u)
        outs = []
        for hq in range(WA_HEADS):
            g = hq // (WA_HEADS // WA_KV_HEADS)
            vrows = slice(g * WA_HEAD_DIM, (g + 1) * WA_HEAD_DIM)
            acc = _dot(jnp.concatenate([vc_ref[vrows, :], jnp.ones((ONES_ROWS, n_ctx), _BF)], axis=0), pc_ref[hq])
            if has_lat:
                acc = acc + _dot(jnp.concatenate([vl_ref[vrows, pl.ds(start, win_keys)],
                                                  jnp.ones((ONES_ROWS, win_keys), _BF)], axis=0), pl_ref[hq])
            l = acc[WA_HEAD_DIM:WA_HEAD_DIM + 1] + jnp.exp2(sink_ref[hq] * LOG2E - ms[hq])
            outs.append(acc[:WA_HEAD_DIM] / l)
        o_ref[rows_of(u), :] = jnp.concatenate(outs, axis=0).T.astype(_BF)

    def body(u, state):
        ms_prev, ms_cur = state
        value_stage(u - 1, ms_prev)
        softmax_stage(ms_cur)
        return ms_cur, scores_stage(u + 1)

    ms_cur = scores_stage(0)
    softmax_stage(ms_cur)
    if n_sub > 1:
        state = lax.fori_loop(1, n_sub - 1, body, (ms_cur, scores_stage(1)))
        ms_prev, ms_cur = state
        value_stage(n_sub - 2, ms_prev)
        softmax_stage(ms_cur)
    value_stage(n_sub - 1, ms_cur)


def _wa_attn(sink, q, k_lat, vt_lat, k_ctx, vt_ctx, *, batch, tq, n_sub):
    t = q.shape[0]
    per_b = t // batch
    step_q = tq * n_sub
    nq = per_b // step_q
    c = k_ctx.shape[1] // batch
    has_lat = k_lat is not None
    in_specs = [pl.BlockSpec(memory_space=pltpu.SMEM),
                pl.BlockSpec((step_q, 256), lambda b, j: (b * nq + j, 0))]
    args = [sink, q]
    seq_len = 0
    if has_lat:
        seq_len = k_lat.shape[1] // batch
        in_specs += [pl.BlockSpec((WA_HEADS, seq_len, LANES), lambda b, j: (0, b, 0)),
                     pl.BlockSpec((128, seq_len), lambda b, j: (0, b))]
        args += [k_lat, vt_lat]
    in_specs += [pl.BlockSpec((WA_HEADS, c, LANES), lambda b, j: (0, b, 0)),
                 pl.BlockSpec((128, c), lambda b, j: (0, b))]
    args += [k_ctx, vt_ctx]
    win_keys = tq + 2 * WINDOW
    scratch = [pltpu.VMEM((WA_HEADS, c, tq), _F32), pltpu.VMEM((WA_HEADS, c, tq), _BF)]
    if has_lat:
        scratch += [pltpu.VMEM((WA_HEADS, win_keys, tq), _F32),
                    pltpu.VMEM((WA_HEADS, win_keys, tq), _BF),
                    pltpu.VMEM((win_keys, tq), _F32)]
    return pl.pallas_call(
        functools.partial(_wa_kernel, has_lat=has_lat, seq_len=seq_len, n_sub=n_sub),
        out_shape=jax.ShapeDtypeStruct((t, 256), _BF),
        grid=(batch, nq),
        in_specs=in_specs,
        out_specs=pl.BlockSpec((step_q, 256), lambda b, j: (b * nq + j, 0)),
        scratch_shapes=scratch,
        compiler_params=_params(2),
        name="wa_attn",
    )(*args)


HALO_SC = 8
HALO_CF = 16
CONV_ROWS = 64
MIX_ROW_GROUPS = 2
SC_PHASES = ((HALO_SC - SC_K // 2) % SUBLANES, (HALO_SC + SC_K // 2) % SUBLANES)
CF_PHASES = tuple(range(1, SUBLANES))


def _mix_out_kernel(h_ref, mod_ref, oa_ref, oc_ref, bg_ref, cx_ref, ud_ref, *rest, halo, seq_tiles):
    if halo:
        cxp_ref, cxn_ref, udp_ref, udn_ref = rest[:4]
        rest = rest[4:]
    (wsc_ref, wcf_ref, bcf_ref, gln_ref, bln_ref, wout_ref, o_ref,
     xsc_ref, xcf_ref, ssc_ref, scf_ref, ob_ref, od_ref) = rest
    tm = h_ref.shape[0]
    width = cx_ref.shape[1]

    if halo:
        i = pl.program_id(0) % seq_tiles
        has_prev = (i != 0).astype(_F32)
        has_next = (i != seq_tiles - 1).astype(_F32)
        xsc_ref[0:HALO_SC, :] = cxp_ref[...] * has_prev
        xsc_ref[HALO_SC + tm:, :] = cxn_ref[...] * has_next
        xcf_ref[0:HALO_CF, :] = udp_ref[...] * has_prev
        xcf_ref[HALO_CF + tm:, :] = udn_ref[...] * has_next
    else:
        xsc_ref[0:HALO_SC, :] = jnp.zeros((HALO_SC, width), _F32)
        xsc_ref[HALO_SC + tm:, :] = jnp.zeros((HALO_SC, width), _F32)
        xcf_ref[0:HALO_CF, :] = jnp.zeros((HALO_CF, width), _F32)
        xcf_ref[HALO_CF + tm:, :] = jnp.zeros((HALO_CF, width), _F32)
    xsc_ref[HALO_SC:HALO_SC + tm, :] = cx_ref[...]
    xcf_ref[HALO_CF:HALO_CF + tm, :] = ud_ref[...]

    for r in range(1, SUBLANES):
        scf_ref[r - 1] = xcf_ref[r:r + scf_ref.shape[1], :]
    for n, r in enumerate(SC_PHASES):
        ssc_ref[n] = xsc_ref[r:r + ssc_ref.shape[1], :]

    def tap(x_ref, copies_ref, phases, row):
        r = row % SUBLANES
        base = row - r
        if r == 0:
            return x_ref[base:base + CONV_ROWS, :]
        return copies_ref[phases.index(r), base:base + CONV_ROWS, :]

    def conv_chunk(r0):
        acc = jnp.zeros((CONV_ROWS, width), _F32)
        for k in range(SC_K):
            row = HALO_SC + r0 + k - SC_K // 2
            acc = acc + tap(xsc_ref, ssc_ref, SC_PHASES, row) * wsc_ref[k:k + 1, :]
        ob_ref[r0:r0 + CONV_ROWS, :] = (bg_ref[r0:r0 + CONV_ROWS, :] * acc).astype(_BF)

        acc = jnp.zeros((CONV_ROWS, width), _F32)
        for k in range(CF_K):
            row = HALO_CF + r0 + k - CF_K // 2
            acc = acc + tap(xcf_ref, scf_ref, CF_PHASES, row) * wcf_ref[k:k + 1, :]
        u = acc + bcf_ref[...]
        mu = jnp.mean(u, axis=-1, keepdims=True)
        uc = u - mu
        var = jnp.mean(uc * uc, axis=-1, keepdims=True)
        y = uc * lax.rsqrt(var + EPS) * gln_ref[...] + bln_ref[...]
        od_ref[r0:r0 + CONV_ROWS, :] = (y * jax.nn.sigmoid(y)).astype(_BF)

    group = tm // MIX_ROW_GROUPS
    for g0 in range(0, tm, group):
        for r0 in range(g0, g0 + group, CONV_ROWS):
            conv_chunk(r0)
        rows = slice(g0, g0 + group)
        mixed = jnp.concatenate([oa_ref[rows, :], ob_ref[rows, :], oc_ref[rows, :], od_ref[rows, :]], axis=1)
        o_ref[rows, :] = h_ref[rows, :] + mod_ref[0, 5:6, :] * _dot(mixed, wout_ref[0])


def _mix_out(h, mod, oa, oc, bg, cx, ud, w, w_out, *, layer, tm, tiles_per_mod, mod_base, seq_tiles):
    t, d = h.shape
    halo = seq_tiles > 1
    if tiles_per_mod is None:
        mod_map = lambda i: (mod_base, 0, 0)
    else:
        mod_map = lambda i: (i // tiles_per_mod, 0, 0)
    row = lambda width: pl.BlockSpec((tm, width), lambda i: (i, 0))
    in_specs = [row(d), pl.BlockSpec((1, N_MOD, d), mod_map), row(256), row(256), row(256), row(256), row(256)]
    args = [h, mod, oa, oc, bg, cx, ud]
    if halo:
        nsc = tm // HALO_SC
        ncf = tm // HALO_CF
        last_sc = t // HALO_SC - 1
        last_cf = t // HALO_CF - 1
        in_specs += [
            pl.BlockSpec((HALO_SC, 256), lambda i: (jnp.maximum(i * nsc - 1, 0), 0)),
            pl.BlockSpec((HALO_SC, 256), lambda i: (jnp.minimum((i + 1) * nsc, last_sc), 0)),
            pl.BlockSpec((HALO_CF, 256), lambda i: (jnp.maximum(i * ncf - 1, 0), 0)),
            pl.BlockSpec((HALO_CF, 256), lambda i: (jnp.minimum((i + 1) * ncf, last_cf), 0)),
        ]
        args += [cx, cx, ud, ud]
    in_specs += [_const_spec((SC_K, 256)), _const_spec((CF_K, 256)), _const_spec((1, 256)),
                 _const_spec((1, 256)), _const_spec((1, 256)), _layer_spec((d, d), layer)]
    args += [w["w_sc"], w["w_cf"], w["b_cf"], w["g_ln"], w["b_ln"], w_out]
    return pl.pallas_call(
        functools.partial(_mix_out_kernel, halo=halo, seq_tiles=seq_tiles),
        out_shape=jax.ShapeDtypeStruct((t, d), _F32),
        grid=(t // tm,),
        in_specs=in_specs,
        out_specs=row(d),
        scratch_shapes=[pltpu.VMEM((tm + 2 * HALO_SC, 256), _F32),
                        pltpu.VMEM((tm + 2 * HALO_CF, 256), _F32),
                        pltpu.VMEM((len(SC_PHASES), tm + HALO_SC, 256), _F32),
                        pltpu.VMEM((len(CF_PHASES), tm + 2 * HALO_CF - SUBLANES, 256), _F32),
                        pltpu.VMEM((tm, 256), _BF),
                        pltpu.VMEM((tm, 256), _BF)],
        compiler_params=_params(1),
        name="mix_out",
    )(*args)


def _swap_halves(w, d_rot):
    m = d_rot // 4
    lead = w.shape[:-1]
    return jnp.flip(w.reshape(lead + (-1, 2, m)), axis=-2).reshape(w.shape)


def _rope_table(seq, d_rot, lanes_before):
    m = d_rot // 4
    d_ax = d_rot // 2
    rows = seq // GRID_W
    inv = ROPE_BASE ** (-jnp.arange(0, d_ax, 2, dtype=_F32) / d_ax)
    ar = jnp.arange(rows, dtype=_F32)[:, None] * inv[None, :]
    ac = jnp.arange(GRID_W, dtype=_F32)[:, None] * inv[None, :]
    reps = (LANES - lanes_before) // d_rot if lanes_before == 0 else 1
    pad = LANES - lanes_before - reps * d_rot

    def table(n, group, fill):
        return jnp.concatenate([jnp.full((n, lanes_before), fill, _F32)] + [jnp.concatenate(group, axis=1)] * reps
                               + [jnp.full((n, pad), fill, _F32)], axis=1)

    zr = jnp.zeros((rows, m), _F32)
    zc = jnp.zeros((GRID_W, m), _F32)
    c_row = table(rows, [jnp.cos(ar), jnp.cos(ar), zr, zr], 0.0)
    s_row = table(rows, [-jnp.sin(ar), jnp.sin(ar), zr, zr], 0.0)
    c_col = table(GRID_W, [zc, zc, jnp.cos(ac), jnp.cos(ac)], 1.0)
    s_col = table(GRID_W, [zc, zc, -jnp.sin(ac), jnp.sin(ac)], 0.0)
    return c_row, c_col, s_row, s_col


def _pack_layer(p, l):
    offs = np.concatenate([[0], np.cumsum(IN_SIZES)])
    o_cq, o_ckv, o_kr, o_sc, o_wq, o_wkv, o_cf, o_end = (int(v) for v in offs)
    scale_a = float((MLA_NOPE + MLA_ROPE) ** -0.5 * LOG2E)
    scale_w = float(WA_HEAD_DIM ** -0.5 * LOG2E)
    w = p["w_in"][l]
    d = w.shape[0]
    zeros = lambda n: jnp.zeros((d, n), _F32)

    kr = w[:, o_kr:o_sc]
    wq = w[:, o_wq:o_wkv] * scale_w
    nk = WA_KV_HEADS * WA_HEAD_DIM
    wk = w[:, o_wkv:o_wkv + nk]
    pad_r = LANES - MLA_NOPE - MLA_ROPE
    w_in = jnp.concatenate([
        w[:, o_cq:o_ckv], zeros(256 - MLA_Q_RANK),
        zeros(MLA_NOPE), kr, zeros(pad_r), w[:, o_ckv:o_kr],
        w[:, o_sc:o_wq],
        wq,
        w[:, o_cf:o_end],
        wk, w[:, o_wkv + nk:o_cf],
    ], axis=1).astype(_BF)

    wu = p["w_mla_uq"][l].reshape(MLA_Q_RANK, MLA_HEADS, MLA_NOPE + MLA_ROPE) * scale_a
    nope, rope = wu[..., :MLA_NOPE], wu[..., MLA_NOPE:]
    zq = lambda n: jnp.zeros((MLA_Q_RANK, MLA_HEADS, n), _F32)
    w_uq = jnp.concatenate([
        jnp.concatenate([nope, rope, zq(pad_r)], axis=-1).reshape(MLA_Q_RANK, MLA_HEADS * LANES),
        jnp.concatenate([zq(MLA_NOPE), _swap_halves(rope, MLA_ROPE), zq(pad_r)],
                        axis=-1).reshape(MLA_Q_RANK, MLA_HEADS * LANES),
    ], axis=1)
    w_uq = jnp.pad(w_uq, ((0, 256 - MLA_Q_RANK), (0, 0))).astype(_BF)
    g_q = jnp.pad(p["g_mla_q"][l], (0, 256 - MLA_Q_RANK)).reshape(1, 256)

    wkv = p["w_mla_ukv"][l].reshape(MLA_KV_RANK, MLA_HEADS, MLA_NOPE + MLA_V)
    k_nope = jnp.concatenate(
        [wkv[..., :MLA_NOPE], jnp.zeros((MLA_KV_RANK, MLA_HEADS, LANES - MLA_NOPE), _F32)], axis=-1)
    w_ukv = jnp.concatenate([k_nope.reshape(MLA_KV_RANK, MLA_HEADS * LANES),
                             wkv[..., MLA_NOPE:].reshape(MLA_KV_RANK, MLA_HEADS * MLA_V)], axis=1).astype(_BF)

    return dict(
        w_in=w_in, w_uq=w_uq, g_q=g_q, w_ukv=w_ukv, g_kv=p["g_mla_kv"][l].reshape(1, MLA_KV_RANK),
        w_sc=p["w_sc_conv"][l], w_cf=p["w_cf_conv"][l], b_cf=p["b_cf_conv"][l].reshape(1, -1),
        g_ln=p["g_cf_ln"][l].reshape(1, -1), b_ln=p["b_cf_ln"][l].reshape(1, -1),
        g_ffn1=p["g_ffn1"][l].reshape(1, -1), g_mix=p["g_mix"][l].reshape(1, -1),
        g_ffn2=p["g_ffn2"][l].reshape(1, -1),
        sink=p["wa_sink"][l],
    )


def kernel(x, c, ctx, c_ctx, w_mod, b_mod, g_ffn1, w1_gate, w1_up, w1_down, g_mix, w_in, g_mla_q, w_mla_uq, g_mla_kv, w_mla_ukv, w_sc_conv, wa_sink, w_cf_conv, b_cf_conv, g_cf_ln, b_cf_ln, w_out, g_ffn2, w2_gate, w2_up, w2_down, g_final):
    b, s, d = x.shape
    n_ctx = ctx.shape[1]
    depth = w_mod.shape[0]
    p = dict(w_in=w_in, g_mla_q=g_mla_q, w_mla_uq=w_mla_uq, g_mla_kv=g_mla_kv, w_mla_ukv=w_mla_ukv,
             w_sc_conv=w_sc_conv, wa_sink=wa_sink, w_cf_conv=w_cf_conv, b_cf_conv=b_cf_conv,
             g_cf_ln=g_cf_ln, b_cf_ln=b_cf_ln, w_out=w_out, g_ffn1=g_ffn1, g_mix=g_mix, g_ffn2=g_ffn2,
             w1_gate=w1_gate, w1_up=w1_up, w1_down=w1_down, w2_gate=w2_gate, w2_up=w2_up, w2_down=w2_down)

    tm = TM_LAT
    seq_tiles = s // tm
    tabs = _rope_table(s, MLA_ROPE, MLA_NOPE) + _rope_table(s, WA_HEAD_DIM, 0)

    mod_rows = 8
    cs = jnp.concatenate([c, c_ctx[None, :], jnp.zeros((mod_rows - b - 1, d), _F32)], axis=0)
    mod_all = _modulation(cs, w_mod, b_mod).reshape(depth, mod_rows, N_MOD, d)

    lat = dict(tm=tm, tiles_per_mod=seq_tiles, mod_base=0)
    cxt = dict(tm=n_ctx, tiles_per_mod=None, mod_base=b)
    cxt_ffn = dict(tm=(b * n_ctx) // FFN_SUBTILES, n_sub=FFN_SUBTILES, tiles_per_mod=None, mod_base=b)

    w1 = (_cast_bf16(w1_gate), _cast_bf16(w1_up), _cast_bf16(w1_down))
    w2 = (_cast_bf16(w2_gate), _cast_bf16(w2_up), _cast_bf16(w2_down))
    w_out_b = _cast_bf16(w_out)

    h_lat = x.reshape(b * s, d)
    h_ctx = ctx.reshape(b * n_ctx, d)
    for l in range(depth):
        last = l == depth - 1
        w = _pack_layer(p, l)
        mod = mod_all[l]

        h_lat = _ffn(h_lat, mod, w["g_ffn1"], *w1, layer=l, which=0, n_sub=FFN_SUBTILES, **lat)
        h_ctx = _ffn(h_ctx, mod, w["g_ffn1"], *w1, layer=l, which=0, **cxt_ffn)

        qa, ka, vat, bg, cx, qw, kw, vwt, ud = _in_proj(h_lat, mod, w["g_mix"], w, tabs, seq_tiles=seq_tiles, **lat)
        qa_c, ka_c, vat_c, bg_c, cx_c, qw_c, kw_c, vwt_c, ud_c = _in_proj(
            h_ctx, mod, w["g_mix"], w, None, seq_tiles=1, **cxt)

        oa = _mla_attn(qa, ka, vat, ka_c, vat_c, batch=b, tq=TQ_MLA)
        oc = _wa_attn(w["sink"], qw, kw, vwt, kw_c, vwt_c, batch=b, tq=TQ_WA, n_sub=WA_SUBTILES)
        h_lat = _mix_out(h_lat, mod, oa, oc, bg, cx, ud, w, w_out_b, layer=l, seq_tiles=seq_tiles, **lat)

        if not last:
            oa_c = _mla_attn(qa_c, None, None, ka_c, vat_c, batch=b, tq=n_ctx)
            oc_c = _wa_attn(w["sink"], qw_c, None, None, kw_c, vwt_c, batch=b, tq=n_ctx, n_sub=1)
            h_ctx = _mix_out(h_ctx, mod, oa_c, oc_c, bg_c, cx_c, ud_c, w, w_out_b, layer=l, seq_tiles=1, **cxt)
            h_ctx = _ffn(h_ctx, mod, w["g_ffn2"], *w2, layer=l, which=2, **cxt_ffn)

        h_lat = _ffn(h_lat, mod, w["g_ffn2"], *w2, layer=l, which=2,
                     g_final=g_final.reshape(1, d) if last else None, n_sub=FFN_SUBTILES, **lat)

    return h_lat.reshape(b, s, d)
```

```python
import functools

import numpy as np
import jax
import jax.numpy as jnp
from jax import lax
from jax.experimental import pallas as pl
from jax.experimental.pallas import tpu as pltpu

GRID_W = 64
ROPE_BASE = 10000.0
EPS = 1e-6
NEG = -1e30
N_MOD = 9
LOG2E = 1.4426950408889634

MLA_HEADS = 4
MLA_NOPE = 64
MLA_ROPE = 32
MLA_V = 64
MLA_Q_RANK = 192
MLA_KV_RANK = 128
SC_WIDTH = 256
SC_K = 3
WA_HEADS = 4
WA_KV_HEADS = 2
WA_HEAD_DIM = 64
WINDOW = 128
CF_WIDTH = 256
CF_K = 31
IN_SIZES = (MLA_Q_RANK, MLA_KV_RANK, MLA_ROPE, 3 * SC_WIDTH,
            WA_HEADS * WA_HEAD_DIM, 2 * WA_KV_HEADS * WA_HEAD_DIM, 2 * CF_WIDTH)

LANES = 128
SUBLANES = 8
MXU_N = 256
SM_ROWS = 32
ONES_ROWS = 16

_BF = jnp.bfloat16
_F32 = jnp.float32

SEG_CQ = 0
SEG_KRC = 256
SEG_SC = 512
SEG_WQ = 1280
SEG_CF = 1536
SEG_WKV = 2048
N_PACK = 2304

TM_LAT = 512
TM_IN_PROJ = 1024
TQ_MLA = 512
TK_MLA = 1024
TQ_WA = 256
WA_SUBTILES = 16
FF_CHUNK = 256
CAST_BLOCK_BYTES = 12 * 1024 * 1024
FFN_SUBTILES = 2
VMEM_LIMIT = 52 * 1024 * 1024


def _params(n_axes, flags=None):
    return pltpu.CompilerParams(dimension_semantics=("arbitrary",) * n_axes,
                                vmem_limit_bytes=VMEM_LIMIT, flags=flags)


def _const_spec(shape):
    nd = len(shape)
    return pl.BlockSpec(shape, lambda *_: (0,) * nd, pipeline_mode=pl.Buffered(1))


def _layer_spec(shape, layer):
    return pl.BlockSpec((1,) + tuple(shape), lambda *_: (layer, 0, 0), pipeline_mode=pl.Buffered(1))


def _cast_kernel(x_ref, o_ref):
    o_ref[...] = x_ref[...].astype(o_ref.dtype)


def _cast_bf16(w):
    depth, rows, cols = w.shape
    tr = rows
    while tr * cols * 4 > CAST_BLOCK_BYTES:
        tr //= 2
    return pl.pallas_call(
        _cast_kernel,
        out_shape=jax.ShapeDtypeStruct(w.shape, _BF),
        grid=(depth, rows // tr),
        in_specs=[pl.BlockSpec((1, tr, cols), lambda l, i: (l, i, 0))],
        out_specs=pl.BlockSpec((1, tr, cols), lambda l, i: (l, i, 0)),
        compiler_params=_params(2),
        name="cast_bf16",
    )(w)


def _dot(a, b):
    return jnp.dot(a, b, preferred_element_type=_F32)


def _dot_nt(a, b):
    return lax.dot_general(a, b, (((1,), (1,)), ((), ())), preferred_element_type=_F32)


def _exp2_bf(d):
    return jnp.exp2(d).astype(_BF)


def _swap_lanes(x, m):
    lane = lax.broadcasted_iota(jnp.int32, x.shape, 1)
    low = (lane % (2 * m)) < m
    return jnp.where(low, pltpu.roll(x, LANES - m, axis=1), pltpu.roll(x, m, axis=1))


def _norm_mod(x, g, shift, scale):
    y = x * lax.rsqrt(jnp.mean(x * x, axis=-1, keepdims=True) + EPS) * g
    return y * (1.0 + scale) + shift


def _mod_kernel(c_ref, w_ref, b_ref, o_ref):
    c = c_ref[...]
    a = c * jax.nn.sigmoid(c)
    rows = a.shape[0]
    a_hi = a.astype(_BF).astype(_F32)
    lhs = jnp.concatenate([a_hi, a - a_hi], axis=0).astype(_BF)
    w = w_ref[0]
    w_hi = w.astype(_BF)
    w_lo = (w - w_hi.astype(_F32)).astype(_BF)
    r = _dot(lhs, w_hi) + _dot(lhs, w_lo)
    o_ref[0] = r[:rows] + r[rows:] + b_ref[0]


def _modulation(cs, w_mod, b_mod):
    depth, d, n = w_mod.shape
    tn = 1024
    rows = cs.shape[0]
    return pl.pallas_call(
        _mod_kernel,
        out_shape=jax.ShapeDtypeStruct((depth, rows, n), _F32),
        grid=(depth, n // tn),
        in_specs=[pl.BlockSpec((rows, d), lambda l, j: (0, 0)),
                  pl.BlockSpec((1, d, tn), lambda l, j: (l, 0, j)),
                  pl.BlockSpec((1, 1, tn), lambda l, j: (l, 0, j))],
        out_specs=pl.BlockSpec((1, rows, tn), lambda l, j: (l, 0, j)),
        compiler_params=_params(2),
        name="modulation",
    )(cs, w_mod, b_mod.reshape(depth, 1, n))


def _ffn_kernel(h_ref, mod_ref, g_ref, wg_ref, wu_ref, wd_ref, *rest, which, final, n_sub):
    if final:
        gf_ref, o_ref, a_ref = rest
    else:
        o_ref, a_ref = rest
    shift = mod_ref[0, 3 * which:3 * which + 1, :]
    scale = mod_ref[0, 3 * which + 1:3 * which + 2, :]
    gate = mod_ref[0, 3 * which + 2:3 * which + 3, :]
    d_ff = wg_ref.shape[2]
    rows_per = h_ref.shape[0] // n_sub
    for sub in range(n_sub):
        rows = slice(sub * rows_per, (sub + 1) * rows_per)
        x = h_ref[rows, :]
        xb = _norm_mod(x, g_ref[...], shift, scale).astype(_BF)
        for j in range(d_ff // FF_CHUNK):
            sl = slice(j * FF_CHUNK, (j + 1) * FF_CHUNK)
            gt = _dot(xb, wg_ref[0, :, sl])
            up = _dot(xb, wu_ref[0, :, sl])
            a_ref[rows, sl] = (gt * jax.nn.sigmoid(gt) * up).astype(_BF)
        y = x + 0.5 * gate * _dot(a_ref[rows, :], wd_ref[0])
        if final:
            y = y * lax.rsqrt(jnp.mean(y * y, axis=-1, keepdims=True) + EPS) * gf_ref[...]
        o_ref[rows, :] = y


def _ffn(h, mod, g, wg, wu, wd, *, layer, which, tm, tiles_per_mod, mod_base, g_final=None, n_sub=1):
    t, d = h.shape
    d_ff = wg.shape[2]
    final = g_final is not None
    tm = tm * n_sub
    if tiles_per_mod is None:
        mod_map = lambda i: (mod_base, 0, 0)
    else:
        mod_map = lambda i: (i // (tiles_per_mod // n_sub), 0, 0)
    in_specs = [pl.BlockSpec((tm, d), lambda i: (i, 0)),
                pl.BlockSpec((1, N_MOD, d), mod_map),
                _const_spec((1, d)),
                _layer_spec((d, d_ff), layer), _layer_spec((d, d_ff), layer), _layer_spec((d_ff, d), layer)]
    args = [h, mod, g, wg, wu, wd]
    if final:
        in_specs.append(_const_spec((1, d)))
        args.append(g_final)
    return pl.pallas_call(
        functools.partial(_ffn_kernel, which=which, final=final, n_sub=n_sub),
        out_shape=jax.ShapeDtypeStruct((t, d), _F32),
        grid=(t // tm,),
        in_specs=in_specs,
        out_specs=pl.BlockSpec((tm, d), lambda i: (i, 0)),
        scratch_shapes=[pltpu.VMEM((tm, d_ff), _BF)],
        compiler_params=_params(1),
        name="ffn",
    )(*args)


def _in_proj_kernel(h_ref, mod_ref, g_ref, win_ref, gq_ref, wuq_ref, gkv_ref, wukv_ref, *rest, rope):
    if rope:
        tab_refs = rest[:8]
        rest = rest[8:]
    qa_ref, ka_ref, vat_ref, bg_ref, cx_ref, qw_ref, kw_ref, vwt_ref, ud_ref, z_ref = rest
    if rope:
        def expand(row_ref, col_ref, row_lanes):
            col = col_ref[...]
            return jnp.concatenate([jnp.where(row_lanes, row_ref[r:r + 1, :], col)
                                    for r in range(h_ref.shape[0] // GRID_W)], axis=0)

        lane = lax.broadcasted_iota(jnp.int32, (GRID_W, LANES), 1)
        rows_a = jnp.abs(2 * (lane - MLA_NOPE) - (MLA_ROPE // 2 - 1)) < MLA_ROPE // 2
        rows_w = (lane % WA_HEAD_DIM) < WA_HEAD_DIM // 2
        ca, sa = expand(tab_refs[0], tab_refs[1], rows_a), expand(tab_refs[2], tab_refs[3], rows_a)
        cw, sw = expand(tab_refs[4], tab_refs[5], rows_w), expand(tab_refs[6], tab_refs[7], rows_w)

    x = h_ref[...]
    xb = _norm_mod(x, g_ref[...], mod_ref[0, 3:4, :], mod_ref[0, 4:5, :]).astype(_BF)

    z_ref[...] = _dot(xb, win_ref[...])

    def seg(lo, width):
        return z_ref[:, lo:lo + width]

    cq = seg(SEG_CQ, 256)
    cqn = cq * lax.rsqrt(jnp.sum(cq * cq, axis=-1, keepdims=True) * (1.0 / MLA_Q_RANK) + EPS) * gq_ref[...]
    qq = _dot(cqn.astype(_BF), wuq_ref[...])
    q = qq[:, :MLA_HEADS * LANES]
    if rope:
        q = (q * jnp.concatenate([ca] * MLA_HEADS, axis=1)
             + qq[:, MLA_HEADS * LANES:] * jnp.concatenate([sa] * MLA_HEADS, axis=1))
    qa_ref[...] = q.astype(_BF)

    krc = seg(SEG_KRC, 256)
    ckv = krc[:, LANES:]
    ckvn = ckv * lax.rsqrt(jnp.mean(ckv * ckv, axis=-1, keepdims=True) + EPS) * gkv_ref[...]
    kv = _dot(ckvn.astype(_BF), wukv_ref[...])
    kr = krc[:, :LANES]
    if rope:
        kr = kr * ca + _swap_lanes(kr, MLA_ROPE // 4) * sa
    for h in range(MLA_HEADS):
        ka_ref[h] = (kv[:, h * LANES:(h + 1) * LANES] + kr).astype(_BF)
    vat_ref[...] = kv[:, MLA_HEADS * LANES:].T.astype(_BF)

    sc = seg(SEG_SC, 768)
    bg_ref[...] = sc[:, :256]
    cx_ref[...] = sc[:, 256:512] * sc[:, 512:]

    qw = seg(SEG_WQ, 256)
    wkv = seg(SEG_WKV, 256)
    kw = wkv[:, :LANES]
    if rope:
        m_w = WA_HEAD_DIM // 4
        qw = jnp.concatenate([qw[:, :LANES] * cw + _swap_lanes(qw[:, :LANES], m_w) * sw,
                              qw[:, LANES:] * cw + _swap_lanes(qw[:, LANES:], m_w) * sw], axis=1)
        kw = kw * cw + _swap_lanes(kw, m_w) * sw
    qw_ref[...] = qw.astype(_BF)
    lane = lax.broadcasted_iota(jnp.int32, kw.shape, 1)
    low = lane < WA_HEAD_DIM
    kw_r = pltpu.roll(kw, WA_HEAD_DIM, axis=1)
    zero = jnp.zeros_like(kw)
    variants = (jnp.where(low, kw, zero), jnp.where(low, zero, kw_r),
                jnp.where(low, kw_r, zero), jnp.where(low, zero, kw))
    for hq in range(WA_HEADS):
        kw_ref[hq] = variants[hq].astype(_BF)
    vwt_ref[...] = wkv[:, LANES:].T.astype(_BF)

    cf = seg(SEG_CF, 512)
    ud_ref[...] = cf[:, :256] * jax.nn.sigmoid(cf[:, 256:])


def _in_proj(h, mod, g, w, tabs, *, tm, tiles_per_mod, mod_base, seq_tiles):
    t, d = h.shape
    rope = tabs is not None
    if tiles_per_mod is None:
        mod_map = lambda i: (mod_base, 0, 0)
    else:
        mod_map = lambda i: (i // tiles_per_mod, 0, 0)
    in_specs = [pl.BlockSpec((tm, d), lambda i: (i, 0)),
                pl.BlockSpec((1, N_MOD, d), mod_map),
                _const_spec((1, d)),
                _const_spec((d, N_PACK)),
                _const_spec((1, 256)), _const_spec((256, 1024)),
                _const_spec((1, 128)), _const_spec((128, 768))]
    args = [h, mod, g, w["w_in"], w["g_q"], w["w_uq"], w["g_kv"], w["w_ukv"]]
    if rope:
        row_tab = pl.BlockSpec((tm // GRID_W, LANES), lambda i: (i % seq_tiles, 0))
        in_specs += [row_tab, _const_spec((GRID_W, LANES))] * 4
        args += list(tabs)
    row = lambda width: pl.BlockSpec((tm, width), lambda i: (i, 0))
    col = lambda height: pl.BlockSpec((height, tm), lambda i: (0, i))
    out_shape = (
        jax.ShapeDtypeStruct((t, 512), _BF),
        jax.ShapeDtypeStruct((MLA_HEADS, t, LANES), _BF),
        jax.ShapeDtypeStruct((256, t), _BF),
        jax.ShapeDtypeStruct((t, 256), _F32),
        jax.ShapeDtypeStruct((t, 256), _F32),
        jax.ShapeDtypeStruct((t, 256), _BF),
        jax.ShapeDtypeStruct((WA_HEADS, t, LANES), _BF),
        jax.ShapeDtypeStruct((128, t), _BF),
        jax.ShapeDtypeStruct((t, 256), _F32),
    )
    heads = lambda n: pl.BlockSpec((n, tm, LANES), lambda i: (0, i, 0))
    out_specs = (row(512), heads(MLA_HEADS), col(256), row(256), row(256), row(256), heads(WA_HEADS),
                 col(128), row(256))
    return pl.pallas_call(
        functools.partial(_in_proj_kernel, rope=rope),
        out_shape=out_shape,
        grid=(t // tm,),
        in_specs=in_specs,
        out_specs=out_specs,
        scratch_shapes=[pltpu.VMEM((tm, N_PACK), _F32)],
        compiler_params=_params(1),
        name="in_proj",
    )(*args)


def _mla_kernel(q_ref, *refs, has_lat, n_lat_tiles):
    if has_lat:
        kl_ref, vl_ref, kc_ref, vc_ref, o_ref, s_ref, p_ref = refs
    else:
        kc_ref, vc_ref, o_ref, s_ref, p_ref = refs
    tq = q_ref.shape[0]
    n_ctx = kc_ref.shape[1]

    def scores_stage(k_ref, key_slice, tk):
        tile_max = []
        for h in range(MLA_HEADS):
            s = _dot_nt(k_ref[h, key_slice, :],
                        q_ref[:, h * LANES:(h + 1) * LANES])
            s_ref[h, 0:tk, :] = s
            tile_max.append(jnp.max(s, axis=0, keepdims=True))
        return tuple(tile_max)

    def softmax_stage(tk, tile_max, ms):
        new_ms, alphas = [], []
        for h in range(MLA_HEADS):
            m_new = jnp.maximum(ms[h], tile_max[h])
            new_ms.append(m_new)
            alphas.append(jnp.exp2(ms[h] - m_new))
            for r0 in range(0, tk, SM_ROWS):
                p_ref[h, r0:r0 + SM_ROWS, :] = _exp2_bf(s_ref[h, r0:r0 + SM_ROWS, :] - m_new)
        return tuple(new_ms), tuple(alphas)

    def value_stage(vt_ref, key_slice, tk, alphas, accs):
        ones = jnp.ones((ONES_ROWS, tk), _BF)
        new = []
        for h in range(MLA_HEADS):
            v1 = jnp.concatenate([vt_ref[h * MLA_V:(h + 1) * MLA_V, key_slice], ones], axis=0)
            new.append(alphas[h] * accs[h] + _dot(v1, p_ref[h, 0:tk, :]))
        return tuple(new)

    ms = tuple(jnp.full((1, tq), NEG, _F32) for _ in range(MLA_HEADS))
    accs = tuple(jnp.zeros((MLA_V + ONES_ROWS, tq), _F32) for _ in range(MLA_HEADS))
    if has_lat:
        def lat(t):
            return pl.ds(pl.multiple_of(t * TK_MLA, TK_MLA), TK_MLA)

        def body(k, state, ctx_next):
            tile_max, ms, accs = state
            ms, alphas = softmax_stage(TK_MLA, tile_max, ms)
            if ctx_next:
                tile_max = scores_stage(kc_ref, slice(None), n_ctx)
            else:
                tile_max = scores_stage(kl_ref, lat(k + 1), TK_MLA)
            return tile_max, ms, value_stage(vl_ref, lat(k), TK_MLA, alphas, accs)

        state = (scores_stage(kl_ref, lat(0), TK_MLA), ms, accs)
        state = lax.fori_loop(0, n_lat_tiles - 1, functools.partial(body, ctx_next=False), state)
        tile_max, ms, accs = body(n_lat_tiles - 1, state, True)
    else:
        tile_max = scores_stage(kc_ref, slice(None), n_ctx)
    ms, alphas = softmax_stage(n_ctx, tile_max, ms)
    accs = value_stage(vc_ref, slice(None), n_ctx, alphas, accs)
    outs = [acc[:MLA_V] / acc[MLA_V:MLA_V + 1] for acc in accs]
    o_ref[...] = jnp.concatenate(outs, axis=0).T.astype(_BF)


def _mla_attn(q, k_lat, vt_lat, k_ctx, vt_ctx, *, batch, tq):
    t = q.shape[0]
    per_b = t // batch
    nq = per_b // tq
    c = k_ctx.shape[1] // batch
    has_lat = k_lat is not None
    in_specs = [pl.BlockSpec((tq, 512), lambda b, j: (b * nq + j, 0))]
    args = [q]
    n_lat_tiles = 0
    if has_lat:
        s = k_lat.shape[1] // batch
        n_lat_tiles = s // TK_MLA
        in_specs += [pl.BlockSpec((MLA_HEADS, s, LANES), lambda b, j: (0, b, 0)),
                     pl.BlockSpec((256, s), lambda b, j: (0, b))]
        args += [k_lat, vt_lat]
    in_specs += [pl.BlockSpec((MLA_HEADS, c, LANES), lambda b, j: (0, b, 0)),
                 pl.BlockSpec((256, c), lambda b, j: (0, b))]
    args += [k_ctx, vt_ctx]
    return pl.pallas_call(
        functools.partial(_mla_kernel, has_lat=has_lat, n_lat_tiles=n_lat_tiles),
        out_shape=jax.ShapeDtypeStruct((t, 256), _BF),
        grid=(batch, nq),
        in_specs=in_specs,
        out_specs=pl.BlockSpec((tq, 256), lambda b, j: (b * nq + j, 0)),
        scratch_shapes=[pltpu.VMEM((MLA_HEADS, max(TK_MLA, c), tq), _F32),
                        pltpu.VMEM((MLA_HEADS, max(TK_MLA, c), tq), _BF)],
        compiler_params=_params(2),
        name="mla_attn",
    )(*args)


def _wa_kernel(sink_ref, q_ref, *refs, has_lat, seq_len, n_sub):
    if has_lat:
        kl_ref, vl_ref, kc_ref, vc_ref, o_ref, sc_ref, pc_ref, sl_ref, pl_ref, bias_ref = refs
    else:
        kc_ref, vc_ref, o_ref, sc_ref, pc_ref = refs
    tq = q_ref.shape[0] // n_sub
    n_ctx = kc_ref.shape[1]
    win_keys = tq + 2 * WINDOW
    step_q0 = pl.program_id(1) * (n_sub * tq)

    def rows_of(u):
        return pl.ds(pl.multiple_of(u * tq, tq), tq)

    def window_of(u):
        q0 = step_q0 + u * tq
        return q0, pl.multiple_of(jnp.clip(q0 - WINDOW, 0, seq_len - win_keys), LANES)

    def scores_stage(u):
        rows = rows_of(u)
        if has_lat:
            q0, start = window_of(u)
            kpos = start + lax.broadcasted_iota(jnp.int32, (win_keys, tq), 0)
            qpos = q0 + lax.broadcasted_iota(jnp.int32, (win_keys, tq), 1)
            bias_ref[...] = jnp.where(jnp.abs(kpos - qpos) <= WINDOW, 0.0, NEG)
        ms = []
        for hq in range(WA_HEADS):
            g = hq // (WA_HEADS // WA_KV_HEADS)
            qpair = q_ref[rows, g * LANES:(g + 1) * LANES]
            s_ctx = _dot_nt(kc_ref[hq], qpair)
            sc_ref[hq] = s_ctx
            m = jnp.maximum(jnp.max(s_ctx, axis=0, keepdims=True), sink_ref[hq] * LOG2E)
            if has_lat:
                s_loc = _dot_nt(kl_ref[hq, pl.ds(start, win_keys), :], qpair) + bias_ref[...]
                sl_ref[hq] = s_loc
                m = jnp.maximum(m, jnp.max(s_loc, axis=0, keepdims=True))
            ms.append(m)
        return tuple(ms)

    def softmax_stage(ms):
        for hq in range(WA_HEADS):
            for r0 in range(0, n_ctx, SM_ROWS):
                pc_ref[hq, r0:r0 + SM_ROWS, :] = _exp2_bf(sc_ref[hq, r0:r0 + SM_ROWS, :] - ms[hq])
            if has_lat:
                for r0 in range(0, win_keys, SM_ROWS):
                    pl_ref[hq, r0:r0 + SM_ROWS, :] = _exp2_bf(sl_ref[hq, r0:r0 + SM_ROWS, :] - ms[hq])

    def value_stage(u, ms):
        if has_lat:
            _, start = window_of(u)
        outs = []
        for hq in range(WA_HEADS):
            g = hq // (WA_HEADS // WA_KV_HEADS)
            vrows = slice(g * WA_HEAD_DIM, (g + 1) * WA_HEAD_DIM)
            acc = _dot(jnp.concatenate([vc_ref[vrows, :], jnp.ones((ONES_ROWS, n_ctx), _BF)], axis=0), pc_ref[hq])
            if has_lat:
                acc = acc + _dot(jnp.concatenate([vl_ref[vrows, pl.ds(start, win_keys)],
                                                  jnp.ones((ONES_ROWS, win_keys), _BF)], axis=0), pl_ref[hq])
            l = acc[WA_HEAD_DIM:WA_HEAD_DIM + 1] + jnp.exp2(sink_ref[hq] * LOG2E - ms[hq])
            outs.append(acc[:WA_HEAD_DIM] / l)
        o_ref[rows_of(u), :] = jnp.concatenate(outs, axis=0).T.astype(_BF)

    def body(u, state):
        ms_prev, ms_cur = state
        value_stage(u - 1, ms_prev)
        softmax_stage(ms_cur)
        return ms_cur, scores_stage(u + 1)

    ms_cur = scores_stage(0)
    softmax_stage(ms_cur)
    if n_sub > 1:
        state = lax.fori_loop(1, n_sub - 1, body, (ms_cur, scores_stage(1)))
        ms_prev, ms_cur = state
        value_stage(n_sub - 2, ms_prev)
        softmax_stage(ms_cur)
    value_stage(n_sub - 1, ms_cur)


def _wa_attn(sink, q, k_lat, vt_lat, k_ctx, vt_ctx, *, batch, tq, n_sub):
    t = q.shape[0]
    per_b = t // batch
    step_q = tq * n_sub
    nq = per_b // step_q
    c = k_ctx.shape[1] // batch
    has_lat = k_lat is not None
    in_specs = [pl.BlockSpec(memory_space=pltpu.SMEM),
                pl.BlockSpec((step_q, 256), lambda b, j: (b * nq + j, 0))]
    args = [sink, q]
    seq_len = 0
    if has_lat:
        seq_len = k_lat.shape[1] // batch
        in_specs += [pl.BlockSpec((WA_HEADS, seq_len, LANES), lambda b, j: (0, b, 0)),
                     pl.BlockSpec((128, seq_len), lambda b, j: (0, b))]
        args += [k_lat, vt_lat]
    in_specs += [pl.BlockSpec((WA_HEADS, c, LANES), lambda b, j: (0, b, 0)),
                 pl.BlockSpec((128, c), lambda b, j: (0, b))]
    args += [k_ctx, vt_ctx]
    win_keys = tq + 2 * WINDOW
    scratch = [pltpu.VMEM((WA_HEADS, c, tq), _F32), pltpu.VMEM((WA_HEADS, c, tq), _BF)]
    if has_lat:
        scratch += [pltpu.VMEM((WA_HEADS, win_keys, tq), _F32),
                    pltpu.VMEM((WA_HEADS, win_keys, tq), _BF),
                    pltpu.VMEM((win_keys, tq), _F32)]
    return pl.pallas_call(
        functools.partial(_wa_kernel, has_lat=has_lat, seq_len=seq_len, n_sub=n_sub),
        out_shape=jax.ShapeDtypeStruct((t, 256), _BF),
        grid=(batch, nq),
        in_specs=in_specs,
        out_specs=pl.BlockSpec((step_q, 256), lambda b, j: (b * nq + j, 0)),
        scratch_shapes=scratch,
        compiler_params=_params(2),
        name="wa_attn",
    )(*args)


HALO_SC = 8
HALO_CF = 16
CONV_ROWS = 64
SC_PHASES = ((HALO_SC - SC_K // 2) % SUBLANES, (HALO_SC + SC_K // 2) % SUBLANES)
CF_PHASES = tuple(range(1, SUBLANES))


def _mix_out_kernel(h_ref, mod_ref, oa_ref, oc_ref, bg_ref, cx_ref, ud_ref, *rest, halo, seq_tiles):
    if halo:
        cxp_ref, cxn_ref, udp_ref, udn_ref = rest[:4]
        rest = rest[4:]
    (wsc_ref, wcf_ref, bcf_ref, gln_ref, bln_ref, wout_ref, o_ref,
     xsc_ref, xcf_ref, ssc_ref, scf_ref, ob_ref, od_ref) = rest
    tm = h_ref.shape[0]
    width = cx_ref.shape[1]

    if halo:
        i = pl.program_id(0) % seq_tiles
        has_prev = (i != 0).astype(_F32)
        has_next = (i != seq_tiles - 1).astype(_F32)
        xsc_ref[0:HALO_SC, :] = cxp_ref[...] * has_prev
        xsc_ref[HALO_SC + tm:, :] = cxn_ref[...] * has_next
        xcf_ref[0:HALO_CF, :] = udp_ref[...] * has_prev
        xcf_ref[HALO_CF + tm:, :] = udn_ref[...] * has_next
    else:
        xsc_ref[0:HALO_SC, :] = jnp.zeros((HALO_SC, width), _F32)
        xsc_ref[HALO_SC + tm:, :] = jnp.zeros((HALO_SC, width), _F32)
        xcf_ref[0:HALO_CF, :] = jnp.zeros((HALO_CF, width), _F32)
        xcf_ref[HALO_CF + tm:, :] = jnp.zeros((HALO_CF, width), _F32)
    xsc_ref[HALO_SC:HALO_SC + tm, :] = cx_ref[...]
    xcf_ref[HALO_CF:HALO_CF + tm, :] = ud_ref[...]

    for r in range(1, SUBLANES):
        scf_ref[r - 1] = xcf_ref[r:r + scf_ref.shape[1], :]
    for n, r in enumerate(SC_PHASES):
        ssc_ref[n] = xsc_ref[r:r + ssc_ref.shape[1], :]

    def tap(x_ref, copies_ref, phases, row):
        r = row % SUBLANES
        base = row - r
        if r == 0:
            return x_ref[base:base + CONV_ROWS, :]
        return copies_ref[phases.index(r), base:base + CONV_ROWS, :]

    for r0 in range(0, tm, CONV_ROWS):
        acc = jnp.zeros((CONV_ROWS, width), _F32)
        for k in range(SC_K):
            row = HALO_SC + r0 + k - SC_K // 2
            acc = acc + tap(xsc_ref, ssc_ref, SC_PHASES, row) * wsc_ref[k:k + 1, :]
        ob_ref[r0:r0 + CONV_ROWS, :] = (bg_ref[r0:r0 + CONV_ROWS, :] * acc).astype(_BF)

        acc = jnp.zeros((CONV_ROWS, width), _F32)
        for k in range(CF_K):
            row = HALO_CF + r0 + k - CF_K // 2
            acc = acc + tap(xcf_ref, scf_ref, CF_PHASES, row) * wcf_ref[k:k + 1, :]
        u = acc + bcf_ref[...]
        mu = jnp.mean(u, axis=-1, keepdims=True)
        uc = u - mu
        var = jnp.mean(uc * uc, axis=-1, keepdims=True)
        y = uc * lax.rsqrt(var + EPS) * gln_ref[...] + bln_ref[...]
        od_ref[r0:r0 + CONV_ROWS, :] = (y * jax.nn.sigmoid(y)).astype(_BF)

    mixed = jnp.concatenate([oa_ref[...], ob_ref[...], oc_ref[...], od_ref[...]], axis=1)
    o_ref[...] = h_ref[...] + mod_ref[0, 5:6, :] * _dot(mixed, wout_ref[0])


def _mix_out(h, mod, oa, oc, bg, cx, ud, w, w_out, *, layer, tm, tiles_per_mod, mod_base, seq_tiles):
    t, d = h.shape
    halo = seq_tiles > 1
    if tiles_per_mod is None:
        mod_map = lambda i: (mod_base, 0, 0)
    else:
        mod_map = lambda i: (i // tiles_per_mod, 0, 0)
    row = lambda width: pl.BlockSpec((tm, width), lambda i: (i, 0))
    in_specs = [row(d), pl.BlockSpec((1, N_MOD, d), mod_map), row(256), row(256), row(256), row(256), row(256)]
    args = [h, mod, oa, oc, bg, cx, ud]
    if halo:
        nsc = tm // HALO_SC
        ncf = tm // HALO_CF
        last_sc = t // HALO_SC - 1
        last_cf = t // HALO_CF - 1
        in_specs += [
            pl.BlockSpec((HALO_SC, 256), lambda i: (jnp.maximum(i * nsc - 1, 0), 0)),
            pl.BlockSpec((HALO_SC, 256), lambda i: (jnp.minimum((i + 1) * nsc, last_sc), 0)),
            pl.BlockSpec((HALO_CF, 256), lambda i: (jnp.maximum(i * ncf - 1, 0), 0)),
            pl.BlockSpec((HALO_CF, 256), lambda i: (jnp.minimum((i + 1) * ncf, last_cf), 0)),
        ]
        args += [cx, cx, ud, ud]
    in_specs += [_const_spec((SC_K, 256)), _const_spec((CF_K, 256)), _const_spec((1, 256)),
                 _const_spec((1, 256)), _const_spec((1, 256)), _layer_spec((d, d), layer)]
    args += [w["w_sc"], w["w_cf"], w["b_cf"], w["g_ln"], w["b_ln"], w_out]
    return pl.pallas_call(
        functools.partial(_mix_out_kernel, halo=halo, seq_tiles=seq_tiles),
        out_shape=jax.ShapeDtypeStruct((t, d), _F32),
        grid=(t // tm,),
        in_specs=in_specs,
        out_specs=row(d),
        scratch_shapes=[pltpu.VMEM((tm + 2 * HALO_SC, 256), _F32),
                        pltpu.VMEM((tm + 2 * HALO_CF, 256), _F32),
                        pltpu.VMEM((len(SC_PHASES), tm + HALO_SC, 256), _F32),
                        pltpu.VMEM((len(CF_PHASES), tm + 2 * HALO_CF - SUBLANES, 256), _F32),
                        pltpu.VMEM((tm, 256), _BF),
                        pltpu.VMEM((tm, 256), _BF)],
        compiler_params=_params(1),
        name="mix_out",
    )(*args)


def _swap_halves(w, d_rot):
    m = d_rot // 4
    lead = w.shape[:-1]
    return jnp.flip(w.reshape(lead + (-1, 2, m)), axis=-2).reshape(w.shape)


def _rope_table(seq, d_rot, lanes_before):
    m = d_rot // 4
    d_ax = d_rot // 2
    rows = seq // GRID_W
    inv = ROPE_BASE ** (-jnp.arange(0, d_ax, 2, dtype=_F32) / d_ax)
    ar = jnp.arange(rows, dtype=_F32)[:, None] * inv[None, :]
    ac = jnp.arange(GRID_W, dtype=_F32)[:, None] * inv[None, :]
    reps = (LANES - lanes_before) // d_rot if lanes_before == 0 else 1
    pad = LANES - lanes_before - reps * d_rot

    def table(n, group, fill):
        return jnp.concatenate([jnp.full((n, lanes_before), fill, _F32)] + [jnp.concatenate(group, axis=1)] * reps
                               + [jnp.full((n, pad), fill, _F32)], axis=1)

    zr = jnp.zeros((rows, m), _F32)
    zc = jnp.zeros((GRID_W, m), _F32)
    c_row = table(rows, [jnp.cos(ar), jnp.cos(ar), zr, zr], 0.0)
    s_row = table(rows, [-jnp.sin(ar), jnp.sin(ar), zr, zr], 0.0)
    c_col = table(GRID_W, [zc, zc, jnp.cos(ac), jnp.cos(ac)], 1.0)
    s_col = table(GRID_W, [zc, zc, -jnp.sin(ac), jnp.sin(ac)], 0.0)
    return c_row, c_col, s_row, s_col


def _pack_layer(p, l):
    offs = np.concatenate([[0], np.cumsum(IN_SIZES)])
    o_cq, o_ckv, o_kr, o_sc, o_wq, o_wkv, o_cf, o_end = (int(v) for v in offs)
    scale_a = float((MLA_NOPE + MLA_ROPE) ** -0.5 * LOG2E)
    scale_w = float(WA_HEAD_DIM ** -0.5 * LOG2E)
    w = p["w_in"][l]
    d = w.shape[0]
    zeros = lambda n: jnp.zeros((d, n), _F32)

    kr = w[:, o_kr:o_sc]
    wq = w[:, o_wq:o_wkv] * scale_w
    nk = WA_KV_HEADS * WA_HEAD_DIM
    wk = w[:, o_wkv:o_wkv + nk]
    pad_r = LANES - MLA_NOPE - MLA_ROPE
    w_in = jnp.concatenate([
        w[:, o_cq:o_ckv], zeros(256 - MLA_Q_RANK),
        zeros(MLA_NOPE), kr, zeros(pad_r), w[:, o_ckv:o_kr],
        w[:, o_sc:o_wq],
        wq,
        w[:, o_cf:o_end],
        wk, w[:, o_wkv + nk:o_cf],
    ], axis=1).astype(_BF)

    wu = p["w_mla_uq"][l].reshape(MLA_Q_RANK, MLA_HEADS, MLA_NOPE + MLA_ROPE) * scale_a
    nope, rope = wu[..., :MLA_NOPE], wu[..., MLA_NOPE:]
    zq = lambda n: jnp.zeros((MLA_Q_RANK, MLA_HEADS, n), _F32)
    w_uq = jnp.concatenate([
        jnp.concatenate([nope, rope, zq(pad_r)], axis=-1).reshape(MLA_Q_RANK, MLA_HEADS * LANES),
        jnp.concatenate([zq(MLA_NOPE), _swap_halves(rope, MLA_ROPE), zq(pad_r)],
                        axis=-1).reshape(MLA_Q_RANK, MLA_HEADS * LANES),
    ], axis=1)
    w_uq = jnp.pad(w_uq, ((0, 256 - MLA_Q_RANK), (0, 0))).astype(_BF)
    g_q = jnp.pad(p["g_mla_q"][l], (0, 256 - MLA_Q_RANK)).reshape(1, 256)

    wkv = p["w_mla_ukv"][l].reshape(MLA_KV_RANK, MLA_HEADS, MLA_NOPE + MLA_V)
    k_nope = jnp.concatenate(
        [wkv[..., :MLA_NOPE], jnp.zeros((MLA_KV_RANK, MLA_HEADS, LANES - MLA_NOPE), _F32)], axis=-1)
    w_ukv = jnp.concatenate([k_nope.reshape(MLA_KV_RANK, MLA_HEADS * LANES),
                             wkv[..., MLA_NOPE:].reshape(MLA_KV_RANK, MLA_HEADS * MLA_V)], axis=1).astype(_BF)

    return dict(
        w_in=w_in, w_uq=w_uq, g_q=g_q, w_ukv=w_ukv, g_kv=p["g_mla_kv"][l].reshape(1, MLA_KV_RANK),
        w_sc=p["w_sc_conv"][l], w_cf=p["w_cf_conv"][l], b_cf=p["b_cf_conv"][l].reshape(1, -1),
        g_ln=p["g_cf_ln"][l].reshape(1, -1), b_ln=p["b_cf_ln"][l].reshape(1, -1),
        g_ffn1=p["g_ffn1"][l].reshape(1, -1), g_mix=p["g_mix"][l].reshape(1, -1),
        g_ffn2=p["g_ffn2"][l].reshape(1, -1),
        sink=p["wa_sink"][l],
    )


def kernel(x, c, ctx, c_ctx, w_mod, b_mod, g_ffn1, w1_gate, w1_up, w1_down, g_mix, w_in, g_mla_q, w_mla_uq, g_mla_kv, w_mla_ukv, w_sc_conv, wa_sink, w_cf_conv, b_cf_conv, g_cf_ln, b_cf_ln, w_out, g_ffn2, w2_gate, w2_up, w2_down, g_final):
    b, s, d = x.shape
    n_ctx = ctx.shape[1]
    depth = w_mod.shape[0]
    p = dict(w_in=w_in, g_mla_q=g_mla_q, w_mla_uq=w_mla_uq, g_mla_kv=g_mla_kv, w_mla_ukv=w_mla_ukv,
             w_sc_conv=w_sc_conv, wa_sink=wa_sink, w_cf_conv=w_cf_conv, b_cf_conv=b_cf_conv,
             g_cf_ln=g_cf_ln, b_cf_ln=b_cf_ln, w_out=w_out, g_ffn1=g_ffn1, g_mix=g_mix, g_ffn2=g_ffn2,
             w1_gate=w1_gate, w1_up=w1_up, w1_down=w1_down, w2_gate=w2_gate, w2_up=w2_up, w2_down=w2_down)

    tm = TM_LAT
    seq_tiles = s // tm
    tabs = _rope_table(s, MLA_ROPE, MLA_NOPE) + _rope_table(s, WA_HEAD_DIM, 0)

    mod_rows = 8
    cs = jnp.concatenate([c, c_ctx[None, :], jnp.zeros((mod_rows - b - 1, d), _F32)], axis=0)
    mod_all = _modulation(cs, w_mod, b_mod).reshape(depth, mod_rows, N_MOD, d)

    lat = dict(tm=tm, tiles_per_mod=seq_tiles, mod_base=0)
    cxt = dict(tm=n_ctx, tiles_per_mod=None, mod_base=b)

    w1 = (_cast_bf16(w1_gate), _cast_bf16(w1_up), _cast_bf16(w1_down))
    w2 = (_cast_bf16(w2_gate), _cast_bf16(w2_up), _cast_bf16(w2_down))
    w_out_b = _cast_bf16(w_out)

    h_lat = x.reshape(b * s, d)
    h_ctx = ctx.reshape(b * n_ctx, d)
    for l in range(depth):
        last = l == depth - 1
        w = _pack_layer(p, l)
        mod = mod_all[l]

        h_lat = _ffn(h_lat, mod, w["g_ffn1"], *w1, layer=l, which=0, n_sub=FFN_SUBTILES, **lat)
        h_ctx = _ffn(h_ctx, mod, w["g_ffn1"], *w1, layer=l, which=0, **cxt)

        qa, ka, vat, bg, cx, qw, kw, vwt, ud = _in_proj(
            h_lat, mod, w["g_mix"], w, tabs, tm=TM_IN_PROJ, tiles_per_mod=s // TM_IN_PROJ, mod_base=0,
            seq_tiles=s // TM_IN_PROJ)
        qa_c, ka_c, vat_c, bg_c, cx_c, qw_c, kw_c, vwt_c, ud_c = _in_proj(
            h_ctx, mod, w["g_mix"], w, None, seq_tiles=1, **cxt)

        oa = _mla_attn(qa, ka, vat, ka_c, vat_c, batch=b, tq=TQ_MLA)
        oc = _wa_attn(w["sink"], qw, kw, vwt, kw_c, vwt_c, batch=b, tq=TQ_WA, n_sub=WA_SUBTILES)
        h_lat = _mix_out(h_lat, mod, oa, oc, bg, cx, ud, w, w_out_b, layer=l, seq_tiles=seq_tiles, **lat)

        if not last:
            oa_c = _mla_attn(qa_c, None, None, ka_c, vat_c, batch=b, tq=n_ctx)
            oc_c = _wa_attn(w["sink"], qw_c, None, None, kw_c, vwt_c, batch=b, tq=n_ctx, n_sub=1)
            h_ctx = _mix_out(h_ctx, mod, oa_c, oc_c, bg_c, cx_c, ud_c, w, w_out_b, layer=l, seq_tiles=1, **cxt)
            h_ctx = _ffn(h_ctx, mod, w["g_ffn2"], *w2, layer=l, which=2, **cxt)

        h_lat = _ffn(h_lat, mod, w["g_ffn2"], *w2, layer=l, which=2,
                     g_final=g_final.reshape(1, d) if last else None, n_sub=FFN_SUBTILES, **lat)

    return h_lat.reshape(b, s, d)
```
